```python
import math
import jax, jax.numpy as jnp
from jax import lax
import numpy as np

D_MODEL = 4096
BATCH = 4
SEQ = 4096
DEPTH = 1

N_META = 16
HEAD_DIM = 64
N_Q_HEADS = 32
N_KV_HEADS = 4
Q_PER_KV = N_Q_HEADS // N_KV_HEADS
ATTN_WIDTH = N_Q_HEADS * HEAD_DIM
KV_WIDTH = N_KV_HEADS * HEAD_DIM
WINDOW = 128
BLOCK = 128
ATTN_SCALE = HEAD_DIM ** -0.5
ROPE_DIM = HEAD_DIM // 4
ROPE_THETA = 500000.0
NEG_INF = -1e30
POOL_WINDOWS = (2, 4, 8, 16)
N_POOL_GROUPS = len(POOL_WINDOWS)
POOL_WIDTH = D_MODEL // 2
POOL_GROUP_WIDTH = POOL_WIDTH // N_POOL_GROUPS
N_BRANCHES = 2
IN_WIDTH = ATTN_WIDTH + 2 * KV_WIDTH + POOL_WIDTH + N_BRANCHES * D_MODEL
SPLITS = [ATTN_WIDTH, ATTN_WIDTH + KV_WIDTH, ATTN_WIDTH + 2 * KV_WIDTH,
          ATTN_WIDTH + 2 * KV_WIDTH + POOL_WIDTH]
D_FF = -(-8 * D_MODEL // 768) * 256
DN_ALPHA = (2 * DEPTH) ** 0.25
DN_BETA = (8 * DEPTH) ** -0.25
LN_EPS = 1e-5

kernel_name = "hybrid_swa_sinks_multiscale_pool_gated_deepnorm"


def layer_norm(x, g, b):
    xf = x.astype(jnp.float32)
    mu = xf.mean(-1, keepdims=True)
    var = jnp.square(xf - mu).mean(-1, keepdims=True)
    y = (xf - mu) * lax.rsqrt(var + LN_EPS)
    return (y * g.astype(jnp.float32) + b.astype(jnp.float32)).astype(x.dtype)


def partial_rope(x, pos):
    half = ROPE_DIM // 2
    inv_freq = ROPE_THETA ** (-jnp.arange(half, dtype=jnp.float32) * 2.0 / ROPE_DIM)
    ang = pos.astype(jnp.float32)[:, None] * inv_freq[None, :]
    cos = jnp.cos(ang)[None, :, None, :]
    sin = jnp.sin(ang)[None, :, None, :]
    xr = x[..., :ROPE_DIM].astype(jnp.float32)
    x1, x2 = xr[..., :half], xr[..., half:]
    rot = jnp.concatenate([x1 * cos - x2 * sin, x2 * cos + x1 * sin], axis=-1).astype(x.dtype)
    return jnp.concatenate([rot, x[..., ROPE_DIM:]], axis=-1)


def _band_blocks(a, nb):
    B = a.shape[0]
    ab = a.reshape(B, nb, BLOCK, N_KV_HEADS, HEAD_DIM)
    prev = jnp.pad(ab[:, :-1], ((0, 0), (1, 0), (0, 0), (0, 0), (0, 0)))
    return jnp.concatenate([prev, ab], axis=2)


def sliding_window_attention(q, k, v, sinks):
    B, T = q.shape[0], q.shape[1]
    lead = (-N_META) % BLOCK
    tail = (-(lead + T)) % BLOCK
    Tp = lead + T + tail
    nb = Tp // BLOCK
    pad = ((0, 0), (lead, tail), (0, 0), (0, 0))
    qb = jnp.pad(q, pad).reshape(B, nb, BLOCK, N_KV_HEADS, Q_PER_KV, HEAD_DIM)
    kb = _band_blocks(jnp.pad(k, pad), nb)
    vb = _band_blocks(jnp.pad(v, pad), nb)
    k_meta, v_meta = k[:, :N_META], v[:, :N_META]

    s_band = jnp.einsum('bnqkgd,bnskd->bnkgqs', qb, kb).astype(jnp.float32) * ATTN_SCALE
    s_meta = jnp.einsum('bnqkgd,bmkd->bnkgqm', qb, k_meta).astype(jnp.float32) * ATTN_SCALE

    blk = jnp.arange(nb)
    q_idx = blk[:, None] * BLOCK + jnp.arange(BLOCK)[None, :]
    k_idx = (blk[:, None] - 1) * BLOCK + jnp.arange(2 * BLOCK)[None, :]
    diff = q_idx[:, :, None] - k_idx[:, None, :]
    band_ok = (diff >= 0) & (diff < WINDOW) & (k_idx[:, None, :] >= lead + N_META)
    meta_ok = q_idx[:, :, None] >= lead + jnp.arange(N_META)[None, None, :]
    s_band = jnp.where(band_ok[None, :, None, None], s_band, NEG_INF)
    s_meta = jnp.where(meta_ok[None, :, None, None], s_meta, NEG_INF)

    sink = sinks.astype(jnp.float32).reshape(N_KV_HEADS, Q_PER_KV)[None, None, :, :, None, None]
    m = jnp.maximum(jnp.maximum(s_band.max(-1, keepdims=True), s_meta.max(-1, keepdims=True)), sink)
    p_band = jnp.exp(s_band - m)
    p_meta = jnp.exp(s_meta - m)
    inv = 1.0 / (p_band.sum(-1, keepdims=True) + p_meta.sum(-1, keepdims=True) + jnp.exp(sink - m))
    o = (jnp.einsum('bnkgqs,bnskd->bnqkgd', (p_band * inv).astype(v.dtype), vb)
         + jnp.einsum('bnkgqm,bmkd->bnqkgd', (p_meta * inv).astype(v.dtype), v_meta))
    return o.reshape(B, Tp, ATTN_WIDTH)[:, lead:lead + T]


def multiscale_pool(u, w_grp, scale):
    B, T = u.shape[0], u.shape[1]
    ug = u.reshape(B, T, N_POOL_GROUPS, POOL_GROUP_WIDTH)
    cs = jnp.cumsum(ug.astype(jnp.float32), axis=1)
    cs = jnp.pad(cs, ((0, 0), (1, 0), (0, 0), (0, 0)))
    t = jnp.arange(T)
    outs = []
    for g, w in enumerate(POOL_WINDOWS):
        csg = cs[:, :, g]
        start = jnp.maximum(t + 1 - w, 0)
        win_sum = csg[:, 1:] - csg[:, start]
        count = jnp.minimum(t + 1, w).astype(jnp.float32)[None, :, None]
        outs.append(win_sum / count - ug[:, :, g].astype(jnp.float32))
    pooled = jnp.stack(outs, axis=2).astype(u.dtype)
    mixed = jnp.einsum('btgc,gcd->btgd', pooled, w_grp)
    return mixed.reshape(B, T, POOL_WIDTH) * scale


def setup_inputs(seed: int = 0) -> dict:
    key = jax.random.key(seed)
    ks = jax.random.split(key, 20)
    f32 = jnp.float32

    def nrm(k, shape, s):
        return jax.random.normal(k, shape, f32) * s

    return {
        "x": nrm(ks[0], (BATCH, SEQ, D_MODEL), 1.0),
        "meta_tokens": nrm(ks[1], (N_META, D_MODEL), 1.0),
        "ln_in_g": 1.0 + nrm(ks[2], (D_MODEL,), 0.02),
        "ln_in_b": nrm(ks[3], (D_MODEL,), 0.02),
        "w_in": nrm(ks[4], (DEPTH, D_MODEL, IN_WIDTH), D_MODEL ** -0.5),
        "b_gate": nrm(ks[5], (DEPTH, N_BRANCHES, D_MODEL), 0.1),
        "attn_sinks": nrm(ks[6], (DEPTH, N_Q_HEADS), 0.5),
        "w_attn_up": nrm(ks[7], (DEPTH, ATTN_WIDTH, D_MODEL), ATTN_WIDTH ** -0.5),
        "w_pool_grp": nrm(ks[8], (DEPTH, N_POOL_GROUPS, POOL_GROUP_WIDTH, POOL_GROUP_WIDTH), POOL_GROUP_WIDTH ** -0.5),
        "pool_scale": 1.0 + nrm(ks[9], (DEPTH, POOL_WIDTH), 0.02),
        "w_pool_up": nrm(ks[10], (DEPTH, POOL_WIDTH, D_MODEL), POOL_WIDTH ** -0.5),
        "w_out": nrm(ks[11], (DEPTH, D_MODEL, D_MODEL), DN_BETA * D_MODEL ** -0.5),
        "ln1_g": 1.0 + nrm(ks[12], (DEPTH, D_MODEL), 0.02),
        "ln1_b": nrm(ks[13], (DEPTH, D_MODEL), 0.02),
        "w_ffn_in": nrm(ks[14], (DEPTH, D_MODEL, 2 * D_FF), D_MODEL ** -0.5),
        "w_ffn_down": nrm(ks[15], (DEPTH, D_FF, D_MODEL), DN_BETA * D_FF ** -0.5),
        "ln2_g": 1.0 + nrm(ks[16], (DEPTH, D_MODEL), 0.02),
        "ln2_b": nrm(ks[17], (DEPTH, D_MODEL), 0.02),
    }


def reference(x, meta_tokens, ln_in_g, ln_in_b, w_in, b_gate, attn_sinks, w_attn_up,
              w_pool_grp, pool_scale, w_pool_up, w_out, ln1_g, ln1_b, w_ffn_in,
              w_ffn_down, ln2_g, ln2_b):
    B = x.shape[0]
    meta = jnp.broadcast_to(meta_tokens[None].astype(x.dtype), (B, N_META, D_MODEL))
    h = layer_norm(jnp.concatenate([meta, x], axis=1), ln_in_g, ln_in_b)
    T = h.shape[1]
    pos = jnp.arange(T)

    for l in range(DEPTH):
        proj = h @ w_in[l]
        q, k, v, u, gate_logits = jnp.split(proj, SPLITS, axis=-1)
        q = partial_rope(q.reshape(B, T, N_Q_HEADS, HEAD_DIM), pos)
        k = partial_rope(k.reshape(B, T, N_KV_HEADS, HEAD_DIM), pos)
        v = v.reshape(B, T, N_KV_HEADS, HEAD_DIM)

        a_out = sliding_window_attention(q, k, v, attn_sinks[l]) @ w_attn_up[l]
        p_out = multiscale_pool(u, w_pool_grp[l], pool_scale[l]) @ w_pool_up[l]

        gates = jax.nn.sigmoid(gate_logits.reshape(B, T, N_BRANCHES, D_MODEL) + b_gate[l])
        mixed = gates[:, :, 0] * a_out + gates[:, :, 1] * p_out
        h = layer_norm(DN_ALPHA * h + mixed @ w_out[l], ln1_g[l], ln1_b[l])

        f_gate, f_up = jnp.split(h @ w_ffn_in[l], 2, axis=-1)
        ffn = (jax.nn.silu(f_gate) * f_up) @ w_ffn_down[l]
        h = layer_norm(DN_ALPHA * h + ffn, ln2_g[l], ln2_b[l])

    return h[:, N_META:]
```

```python
import functools

import jax
import jax.numpy as jnp
from jax import lax
from jax.experimental import pallas as pl
from jax.experimental.pallas import tpu as pltpu

F32 = jnp.float32
BF16 = jnp.bfloat16

D_MODEL = 4096
N_META = 16
HEAD_DIM = 64
N_Q_HEADS = 32
N_KV_HEADS = 4
Q_PER_KV = N_Q_HEADS // N_KV_HEADS
ATTN_WIDTH = N_Q_HEADS * HEAD_DIM
KV_WIDTH = N_KV_HEADS * HEAD_DIM
WINDOW = 128
ATTN_SCALE = HEAD_DIM ** -0.5
ROPE_DIM = HEAD_DIM // 4
ROPE_HALF = ROPE_DIM // 2
ROPE_THETA = 500000.0
NEG_INF = -1e30
POOL_WINDOWS = (2, 4, 8, 16)
POOL_WIDTH = D_MODEL // 2
POOL_GROUP_WIDTH = POOL_WIDTH // len(POOL_WINDOWS)
D_FF = -(-8 * D_MODEL // 768) * 256
DEPTH = 1
DN_ALPHA = (2 * DEPTH) ** 0.25
LN_EPS = 1e-5

LANES = 128
VMEM_LIMIT = 56 * 1024 * 1024


def _params(n_grid):
    return pltpu.CompilerParams(dimension_semantics=("arbitrary",) * n_grid, vmem_limit_bytes=VMEM_LIMIT)


def _ln_body(x_ref, g_ref, b_ref, *o_refs):
    x = x_ref[...]
    mu = jnp.mean(x, axis=-1, keepdims=True)
    xc = x - mu
    var = jnp.mean(xc * xc, axis=-1, keepdims=True)
    y = xc * lax.rsqrt(var + LN_EPS) * g_ref[...] + b_ref[...]
    for o_ref in o_refs:
        o_ref[...] = y.astype(o_ref.dtype)


def _layer_norm(x, g, b, out_dtypes, bm):
    m, d = x.shape
    row = pl.BlockSpec((bm, d), lambda i: (i, 0))
    vec = pl.BlockSpec((1, d), lambda i: (0, 0))
    return pl.pallas_call(
        _ln_body,
        grid=(m // bm,),
        in_specs=[row, vec, vec],
        out_specs=[row for _ in out_dtypes],
        out_shape=[jax.ShapeDtypeStruct((m, d), dt) for dt in out_dtypes],
        compiler_params=_params(1),
        name="layer_norm",
    )(x, g.reshape(1, d), b.reshape(1, d))


def _rope_tables(pos):
    inv_freq = ROPE_THETA ** (-jnp.arange(ROPE_HALF, dtype=F32) * 2.0 / ROPE_DIM)
    ang = pos.astype(F32)[:, None] * inv_freq[None, :]
    lane = jnp.arange(LANES) % HEAD_DIM
    cos = jnp.cos(ang)[:, lane % ROPE_HALF]
    sin = jnp.sin(ang)[:, lane % ROPE_HALF]
    c = jnp.where(lane < ROPE_DIM, cos, 1.0)
    s_lo = jnp.where(lane < ROPE_HALF, -sin, 0.0)
    s_hi = jnp.where((lane >= ROPE_HALF) & (lane < ROPE_DIM), sin, 0.0)
    return jnp.stack([c, s_lo, s_hi])


def _rope_chunk(x, tab_ref):
    return (x * tab_ref[0]
            + pltpu.roll(x, LANES - ROPE_HALF, 1) * tab_ref[1]
            + pltpu.roll(x, ROPE_HALF, 1) * tab_ref[2])


def _proj_rope_body(x_ref, w_ref, tab_ref, o_ref, *, rope_cols, scale):
    acc = jnp.dot(x_ref[...], w_ref[...], preferred_element_type=F32)
    bn = acc.shape[1]
    for c in range(bn // LANES):
        blk = acc[:, c * LANES:(c + 1) * LANES]
        if c * LANES < rope_cols:
            blk = _rope_chunk(blk, tab_ref)
        if scale != 1.0:
            blk = blk * scale
        o_ref[:, c * LANES:(c + 1) * LANES] = blk.astype(o_ref.dtype)


def _proj_rope(x, w, tab, rope_cols, scale, bm, bn):
    m, k = x.shape
    n = w.shape[1]
    t_blocks = tab.shape[1] // bm
    return pl.pallas_call(
        functools.partial(_proj_rope_body, rope_cols=rope_cols, scale=scale),
        grid=(n // bn, m // bm),
        in_specs=[pl.BlockSpec((bm, k), lambda j, i: (i, 0)),
                  pl.BlockSpec((k, bn), lambda j, i: (0, j)),
                  pl.BlockSpec((3, bm, LANES), lambda j, i: (0, i % t_blocks, 0))],
        out_specs=pl.BlockSpec((bm, bn), lambda j, i: (i, j)),
        out_shape=jax.ShapeDtypeStruct((m, n), BF16),
        compiler_params=_params(2),
        name="proj_rope",
    )(x, w, tab)


def _proj_plain_body(x_ref, w_ref, o_ref):
    o_ref[...] = jnp.dot(x_ref[...], w_ref[...], preferred_element_type=F32).astype(o_ref.dtype)


def _proj_plain(x, w, out_dtype, bm, bn):
    m, k = x.shape
    n = w.shape[1]
    return pl.pallas_call(
        _proj_plain_body,
        grid=(n // bn, m // bm),
        in_specs=[pl.BlockSpec((bm, k), lambda j, i: (i, 0)),
                  pl.BlockSpec((k, bn), lambda j, i: (0, j))],
        out_specs=pl.BlockSpec((bm, bn), lambda j, i: (i, j)),
        out_shape=jax.ShapeDtypeStruct((m, n), out_dtype),
        compiler_params=_params(2),
        name="proj_plain",
    )(x, w)


def _proj_gate_body(x_ref, w_ref, b_ref, o_ref):
    acc = jnp.dot(x_ref[...], w_ref[...], preferred_element_type=F32)
    o_ref[...] = jax.nn.sigmoid(acc + b_ref[...]).astype(o_ref.dtype)


def _proj_gate(x, w, b, bm, bn):
    m, k = x.shape
    n = w.shape[1]
    return pl.pallas_call(
        _proj_gate_body,
        grid=(n // bn, m // bm),
        in_specs=[pl.BlockSpec((bm, k), lambda j, i: (i, 0)),
                  pl.BlockSpec((k, bn), lambda j, i: (0, j)),
                  pl.BlockSpec((1, bn), lambda j, i: (0, j))],
        out_specs=pl.BlockSpec((bm, bn), lambda j, i: (i, j)),
        out_shape=jax.ShapeDtypeStruct((m, n), BF16),
        compiler_params=_params(2),
        name="proj_gate",
    )(x, w, b)


_NT = (((1,), (1,)), ((), ()))


def _attn_body(sink_ref, q_ref, kvp_ref, kvc_ref, kvm_ref, o_ref):
    i = pl.program_id(1)
    row = lax.broadcasted_iota(jnp.int32, (WINDOW, WINDOW), 0)
    col = lax.broadcasted_iota(jnp.int32, (WINDOW, WINDOW), 1)
    in_cur = col <= row
    has_prev = i > 0
    for kvh in range(N_KV_HEADS):
        ks = slice(kvh * HEAD_DIM, (kvh + 1) * HEAD_DIM)
        vs = slice(KV_WIDTH + kvh * HEAD_DIM, KV_WIDTH + (kvh + 1) * HEAD_DIM)
        k_band = jnp.concatenate([kvp_ref[:, ks], kvc_ref[:, ks]], axis=0)
        v_band = jnp.concatenate([kvp_ref[:, vs], kvc_ref[:, vs]], axis=0)
        k_meta = kvm_ref[:, ks]
        v_meta = kvm_ref[:, vs]
        for g in range(Q_PER_KV):
            h = kvh * Q_PER_KV + g
            q = q_ref[:, h * HEAD_DIM:(h + 1) * HEAD_DIM]
            s2 = lax.dot_general(q, k_band, _NT, preferred_element_type=F32)
            s = jnp.where(in_cur, s2[:, WINDOW:], jnp.where(has_prev, s2[:, :WINDOW], NEG_INF))
            sm = lax.dot_general(q, k_meta, _NT, preferred_element_type=F32)
            sink = sink_ref[h]
            mx = jnp.maximum(jnp.maximum(jnp.max(s, axis=-1, keepdims=True),
                                         jnp.max(sm, axis=-1, keepdims=True)), sink)
            p = jnp.exp(s - mx)
            pm = jnp.exp(sm - mx)
            inv = 1.0 / (jnp.sum(p, axis=-1, keepdims=True) + jnp.sum(pm, axis=-1, keepdims=True)
                         + jnp.exp(sink - mx))
            pn = (p * inv).astype(BF16)
            zero = jnp.zeros_like(pn)
            p2 = jnp.concatenate([jnp.where(in_cur, zero, pn), jnp.where(in_cur, pn, zero)], axis=1)
            o = (jnp.dot(p2, v_band, preferred_element_type=F32)
                 + jnp.dot((pm * inv).astype(BF16), v_meta, preferred_element_type=F32))
            o_ref[:, h * HEAD_DIM:(h + 1) * HEAD_DIM] = o.astype(o_ref.dtype)


def _attention(q, kv, kv_meta, sinks, batch, seq):
    nb = seq // WINDOW
    return pl.pallas_call(
        _attn_body,
        grid=(batch, nb),
        in_specs=[pl.BlockSpec(memory_space=pltpu.SMEM),
                  pl.BlockSpec((WINDOW, ATTN_WIDTH), lambda b, i: (b * nb + i, 0)),
                  pl.BlockSpec((WINDOW, 2 * KV_WIDTH), lambda b, i: (b * nb + jnp.maximum(i - 1, 0), 0)),
                  pl.BlockSpec((WINDOW, 2 * KV_WIDTH), lambda b, i: (b * nb + i, 0)),
                  pl.BlockSpec((N_META, 2 * KV_WIDTH), lambda b, i: (0, 0))],
        out_specs=pl.BlockSpec((WINDOW, ATTN_WIDTH), lambda b, i: (b * nb + i, 0)),
        out_shape=jax.ShapeDtypeStruct((batch * seq, ATTN_WIDTH), BF16),
        compiler_params=_params(2),
        name="swa_attention",
    )(sinks, q, kv, kv, kv_meta)


POOL_HALO = 16


def _pool_body(up_ref, uc_ref, um_ref, w_ref, sc_ref, o_ref, ext_ref, *, bt):
    i = pl.program_id(1)
    ext_ref[0:POOL_HALO, :] = jnp.where(i == 0, um_ref[...], up_ref[...])
    ext_ref[POOL_HALO:, :] = uc_ref[...]
    for g, w in enumerate(POOL_WINDOWS):
        cs = slice(g * POOL_GROUP_WIDTH, (g + 1) * POOL_GROUP_WIDTH)
        cur = uc_ref[:, cs]
        win = cur
        for j in range(1, w):
            win = win + ext_ref[POOL_HALO - j:POOL_HALO - j + bt, cs]
        pooled = win / float(w) - cur
        mixed = jnp.dot(pooled.astype(BF16), w_ref[g], preferred_element_type=F32)
        o_ref[:, cs] = (mixed * sc_ref[:, cs]).astype(o_ref.dtype)


def _pool(u, u_meta, w_grp, scale, batch, seq, bt):
    nb = seq // bt
    halo_per_blk = bt // POOL_HALO
    return pl.pallas_call(
        functools.partial(_pool_body, bt=bt),
        grid=(batch, nb),
        in_specs=[pl.BlockSpec((POOL_HALO, POOL_WIDTH),
                               lambda b, i: (jnp.maximum((b * nb + i) * halo_per_blk - 1, 0), 0)),
                  pl.BlockSpec((bt, POOL_WIDTH), lambda b, i: (b * nb + i, 0)),
                  pl.BlockSpec((N_META, POOL_WIDTH), lambda b, i: (0, 0)),
                  pl.BlockSpec((len(POOL_WINDOWS), POOL_GROUP_WIDTH, POOL_GROUP_WIDTH), lambda b, i: (0, 0, 0)),
                  pl.BlockSpec((1, POOL_WIDTH), lambda b, i: (0, 0))],
        out_specs=pl.BlockSpec((bt, POOL_WIDTH), lambda b, i: (b * nb + i, 0)),
        out_shape=jax.ShapeDtypeStruct((batch * seq, POOL_WIDTH), BF16),
        scratch_shapes=[pltpu.VMEM((POOL_HALO + bt, POOL_WIDTH), F32)],
        compiler_params=_params(2),
        name="multiscale_pool",
    )(u, u, u_meta, w_grp, scale)


def _up_body(a_ref, p_ref, wa_ref, wp_ref, ga_ref, gb_ref, o_ref):
    a = jnp.dot(a_ref[...], wa_ref[...], preferred_element_type=F32)
    p = jnp.dot(p_ref[...], wp_ref[...], preferred_element_type=F32)
    o_ref[...] = (ga_ref[...].astype(F32) * a + gb_ref[...].astype(F32) * p).astype(o_ref.dtype)


def _up_merge(attn, pool, wa, wp, gates, bm, bn):
    m = attn.shape[0]
    n = wa.shape[1]
    nj = n // bn
    return pl.pallas_call(
        _up_body,
        grid=(nj, m // bm),
        in_specs=[pl.BlockSpec((bm, attn.shape[1]), lambda j, i: (i, 0)),
                  pl.BlockSpec((bm, pool.shape[1]), lambda j, i: (i, 0)),
                  pl.BlockSpec((wa.shape[0], bn), lambda j, i: (0, j)),
                  pl.BlockSpec((wp.shape[0], bn), lambda j, i: (0, j)),
                  pl.BlockSpec((bm, bn), lambda j, i: (i, j)),
                  pl.BlockSpec((bm, bn), lambda j, i: (i, j + nj))],
        out_specs=pl.BlockSpec((bm, bn), lambda j, i: (i, j)),
        out_shape=jax.ShapeDtypeStruct((m, n), BF16),
        compiler_params=_params(2),
        name="up_merge",
    )(attn, pool, wa, wp, gates, gates)


def _resid_mm_body(x_ref, w_ref, h_ref, o_ref):
    acc = jnp.dot(x_ref[...], w_ref[...], preferred_element_type=F32)
    o_ref[...] = DN_ALPHA * h_ref[...] + acc


def _resid_matmul(x, w, h, bm, bn):
    m, k = x.shape
    n = w.shape[1]
    return pl.pallas_call(
        _resid_mm_body,
        grid=(n // bn, m // bm),
        in_specs=[pl.BlockSpec((bm, k), lambda j, i: (i, 0)),
                  pl.BlockSpec((k, bn), lambda j, i: (0, j)),
                  pl.BlockSpec((bm, bn), lambda j, i: (i, j))],
        out_specs=pl.BlockSpec((bm, bn), lambda j, i: (i, j)),
        out_shape=jax.ShapeDtypeStruct((m, n), F32),
        compiler_params=_params(2),
        name="resid_matmul",
    )(x, w, h)


def _swiglu_body(x_ref, wg_ref, wu_ref, o_ref):
    x = x_ref[...]
    g = jnp.dot(x, wg_ref[...], preferred_element_type=F32)
    u = jnp.dot(x, wu_ref[...], preferred_element_type=F32)
    o_ref[...] = (jax.nn.silu(g) * u).astype(o_ref.dtype)


def _swiglu_in(x, w, d_ff, bm, bn):
    m, k = x.shape
    nj = d_ff // bn
    return pl.pallas_call(
        _swiglu_body,
        grid=(nj, m // bm),
        in_specs=[pl.BlockSpec((bm, k), lambda j, i: (i, 0)),
                  pl.BlockSpec((k, bn), lambda j, i: (0, j)),
                  pl.BlockSpec((k, bn), lambda j, i: (0, j + nj))],
        out_specs=pl.BlockSpec((bm, bn), lambda j, i: (i, j)),
        out_shape=jax.ShapeDtypeStruct((m, d_ff), BF16),
        compiler_params=_params(2),
        name="swiglu_in",
    )(x, w, w)


def kernel(x, meta_tokens, ln_in_g, ln_in_b, w_in, b_gate, attn_sinks, w_attn_up, w_pool_grp, pool_scale,
           w_pool_up, w_out, ln1_g, ln1_b, w_ffn_in, w_ffn_down, ln2_g, ln2_b):
    batch, seq, d = x.shape
    m = batch * seq
    a0, a1, a2, a3 = ATTN_WIDTH, ATTN_WIDTH + KV_WIDTH, ATTN_WIDTH + 2 * KV_WIDTH, ATTN_WIDTH + 2 * KV_WIDTH + POOL_WIDTH

    w_q = w_in[0, :, :a0].astype(BF16)
    w_kv = w_in[0, :, a0:a2].astype(BF16)
    w_u = w_in[0, :, a2:a3].astype(BF16)
    w_g = w_in[0, :, a3:].astype(BF16)

    tab_x = _rope_tables(jnp.arange(N_META, N_META + seq))
    tab_m = _rope_tables(jnp.arange(N_META))

    h0, h0b = _layer_norm(x.reshape(m, d), ln_in_g, ln_in_b, (F32, BF16), 256)
    (hmb,) = _layer_norm(meta_tokens, ln_in_g, ln_in_b, (BF16,), N_META)

    q = _proj_rope(h0b, w_q, tab_x, ATTN_WIDTH, ATTN_SCALE, 1024, 1024)
    kv = _proj_rope(h0b, w_kv, tab_x, KV_WIDTH, 1.0, 1024, 2 * KV_WIDTH)
    kv_meta = _proj_rope(hmb, w_kv, tab_m, KV_WIDTH, 1.0, N_META, 2 * KV_WIDTH)
    u = _proj_plain(h0b, w_u, F32, 1024, 1024)
    u_meta = _proj_plain(hmb, w_u, F32, N_META, 1024)
    gates = _proj_gate(h0b, w_g, b_gate[0].reshape(1, 2 * d), 1024, 1024)

    attn = _attention(q, kv, kv_meta, attn_sinks[0], batch, seq)
    pool = _pool(u, u_meta, w_pool_grp[0].astype(BF16), pool_scale[0].reshape(1, POOL_WIDTH), batch, seq, 256)

    mixed = _up_merge(attn, pool, w_attn_up[0].astype(BF16), w_pool_up[0].astype(BF16), gates, 512, 1024)
    y1 = _resid_matmul(mixed, w_out[0].astype(BF16), h0, 1024, 512)
    h1, h1b = _layer_norm(y1, ln1_g[0], ln1_b[0], (F32, BF16), 256)

    act = _swiglu_in(h1b, w_ffn_in[0].astype(BF16), D_FF, 1024, 256)
    y2 = _resid_matmul(act, w_ffn_down[0].astype(BF16), h1, 512, 512)
    (out,) = _layer_norm(y2, ln2_g[0], ln2_b[0], (F32,), 256)
    return out.reshape(batch, seq, d)
```

```python
import functools

import jax
import jax.numpy as jnp
from jax import lax
from jax.experimental import pallas as pl
from jax.experimental.pallas import tpu as pltpu

F32 = jnp.float32
BF16 = jnp.bfloat16

D_MODEL = 4096
N_META = 16
HEAD_DIM = 64
N_Q_HEADS = 32
N_KV_HEADS = 4
Q_PER_KV = N_Q_HEADS // N_KV_HEADS
ATTN_WIDTH = N_Q_HEADS * HEAD_DIM
KV_WIDTH = N_KV_HEADS * HEAD_DIM
WINDOW = 128
ATTN_SCALE = HEAD_DIM ** -0.5
ROPE_DIM = HEAD_DIM // 4
ROPE_HALF = ROPE_DIM // 2
ROPE_THETA = 500000.0
NEG_INF = -1e30
POOL_WINDOWS = (2, 4, 8, 16)
POOL_WIDTH = D_MODEL // 2
POOL_GROUP_WIDTH = POOL_WIDTH // len(POOL_WINDOWS)
IN_SPLITS = (0, ATTN_WIDTH, ATTN_WIDTH + KV_WIDTH, ATTN_WIDTH + 2 * KV_WIDTH,
             ATTN_WIDTH + 2 * KV_WIDTH + POOL_WIDTH)
D_FF = -(-8 * D_MODEL // 768) * 256
DEPTH = 1
DN_ALPHA = (2 * DEPTH) ** 0.25
LN_EPS = 1e-5

LANES = 128
VMEM_LIMIT = 56 * 1024 * 1024

ROW_TILE = 1024
COL_TILE = 512
FFN_COL_TILE = 256
DOWN_ROW_TILE = 512
LN_ROW_TILE = 256
POOL_ROW_TILE = 256


def _params(n_grid):
    return pltpu.CompilerParams(dimension_semantics=("arbitrary",) * n_grid, vmem_limit_bytes=VMEM_LIMIT)


MM_ROW_CHUNK = 256


def _row_chunks(bm):
    chunk = min(bm, MM_ROW_CHUNK)
    return [slice(r, r + chunk) for r in range(0, bm, chunk)]


def _cast_weights_once(w_refs, wb_refs):
    @pl.when(pl.program_id(1) == 0)
    def _():
        for w_ref, wb_ref in zip(w_refs, wb_refs):
            wb_ref[...] = w_ref[...].astype(BF16)


def _normalize(x, mu, rstd, g, b):
    return (x - mu) * rstd * g + b


def _ln_body(x_ref, g_ref, b_ref, y_ref, *stat_refs):
    x = x_ref[...]
    mu = jnp.mean(x, axis=-1, keepdims=True)
    xc = x - mu
    rstd = lax.rsqrt(jnp.mean(xc * xc, axis=-1, keepdims=True) + LN_EPS)
    y_ref[...] = _normalize(x, mu, rstd, g_ref[...], b_ref[...]).astype(y_ref.dtype)
    if stat_refs:
        mu_ref, rstd_ref = stat_refs
        mu_ref[...] = jnp.broadcast_to(mu, mu_ref.shape)
        rstd_ref[...] = jnp.broadcast_to(rstd, rstd_ref.shape)


def _layer_norm(x, g, b, out_dtype, bm, with_stats):
    m, d = x.shape
    row = pl.BlockSpec((bm, d), lambda i: (i, 0))
    vec = pl.BlockSpec((1, d), lambda i: (0, 0))
    stat = pl.BlockSpec((bm, LANES), lambda i: (i, 0))
    n_stats = 2 if with_stats else 0
    return pl.pallas_call(
        _ln_body,
        grid=(m // bm,),
        in_specs=[row, vec, vec],
        out_specs=[row] + [stat] * n_stats,
        out_shape=[jax.ShapeDtypeStruct((m, d), out_dtype)] + [jax.ShapeDtypeStruct((m, LANES), F32)] * n_stats,
        compiler_params=_params(1),
        name="layer_norm",
    )(x, g.reshape(1, d), b.reshape(1, d))


def _rope_tables(pos):
    inv_freq = ROPE_THETA ** (-jnp.arange(ROPE_HALF, dtype=F32) * 2.0 / ROPE_DIM)
    ang = pos.astype(F32)[:, None] * inv_freq[None, :]
    lane = jnp.arange(LANES) % HEAD_DIM
    cos = jnp.cos(ang)[:, lane % ROPE_HALF]
    sin = jnp.sin(ang)[:, lane % ROPE_HALF]
    c = jnp.where(lane < ROPE_DIM, cos, 1.0)
    s_lo = jnp.where(lane < ROPE_HALF, -sin, 0.0)
    s_hi = jnp.where((lane >= ROPE_HALF) & (lane < ROPE_DIM), sin, 0.0)
    return jnp.stack([c, s_lo, s_hi])


def _rope_chunk(x, tab_ref, rs):
    return (x * tab_ref[0, rs, :]
            + pltpu.roll(x, LANES - ROPE_HALF, 1) * tab_ref[1, rs, :]
            + pltpu.roll(x, ROPE_HALF, 1) * tab_ref[2, rs, :])


def _proj_rope_body(x_ref, w_ref, tab_ref, o_ref, wb_ref, *, rope_cols, scale):
    _cast_weights_once([w_ref], [wb_ref])
    for rs in _row_chunks(x_ref.shape[0]):
        acc = jnp.dot(x_ref[rs, :], wb_ref[...], preferred_element_type=F32)
        for c in range(acc.shape[1] // LANES):
            blk = acc[:, c * LANES:(c + 1) * LANES]
            if c * LANES < rope_cols:
                blk = _rope_chunk(blk, tab_ref, rs)
            if scale != 1.0:
                blk = blk * scale
            o_ref[rs, c * LANES:(c + 1) * LANES] = blk.astype(o_ref.dtype)


def _proj_plain_body(x_ref, w_ref, o_ref, wb_ref):
    _cast_weights_once([w_ref], [wb_ref])
    for rs in _row_chunks(x_ref.shape[0]):
        o_ref[rs, :] = jnp.dot(x_ref[rs, :], wb_ref[...], preferred_element_type=F32).astype(o_ref.dtype)


def _proj_gate_body(x_ref, w_ref, b_ref, o_ref, wb_ref):
    _cast_weights_once([w_ref], [wb_ref])
    for rs in _row_chunks(x_ref.shape[0]):
        acc = jnp.dot(x_ref[rs, :], wb_ref[...], preferred_element_type=F32)
        o_ref[rs, :] = jax.nn.sigmoid(acc + b_ref[...]).astype(o_ref.dtype)


def _in_proj(body, x, w_in, col0, n, extra, extra_specs, out_dtype, bm, name):
    m, k = x.shape
    bn = COL_TILE
    j0 = col0 // bn
    return pl.pallas_call(
        body,
        grid=(n // bn, m // bm),
        in_specs=[pl.BlockSpec((bm, k), lambda j, i: (i, 0)),
                  pl.BlockSpec((None, k, bn), lambda j, i: (0, 0, j0 + j))] + extra_specs,
        out_specs=pl.BlockSpec((bm, bn), lambda j, i: (i, j)),
        out_shape=jax.ShapeDtypeStruct((m, n), out_dtype),
        scratch_shapes=[pltpu.VMEM((k, bn), BF16)],
        compiler_params=_params(2),
        name=name,
    )(x, w_in, *extra)


def _rope_spec(tab, bm):
    t_blocks = tab.shape[1] // bm
    return pl.BlockSpec((3, bm, LANES), lambda j, i: (0, i % t_blocks, 0))


_NT = (((1,), (1,)), ((), ()))


HEADS_PER_DOT = 4
HEADS_PER_VREG = LANES // HEAD_DIM


def _attn_body(sink_ref, q_ref, kvp_ref, kvc_ref, kvm_ref, o_ref):
    i = pl.program_id(1)
    key = lax.broadcasted_iota(jnp.int32, (WINDOW, WINDOW), 0)
    qry = lax.broadcasted_iota(jnp.int32, (WINDOW, WINDOW), 1)
    in_cur = key <= qry
    prev_bias = jnp.where(i > 0, 0.0, NEG_INF).astype(F32)
    lane_slot = lax.broadcasted_iota(jnp.int32, (1, LANES), 1) // HEAD_DIM
    meta_pad = jnp.zeros((LANES - N_META, LANES), F32)

    for kvh in range(N_KV_HEADS):
        blk, slot = divmod(kvh, HEADS_PER_VREG)
        kcols = slice(blk * LANES, (blk + 1) * LANES)
        vcols = slice(KV_WIDTH + blk * LANES, KV_WIDTH + (blk + 1) * LANES)

        k2 = jnp.concatenate([kvp_ref[:, kcols], kvc_ref[:, kcols], kvm_ref[:, kcols]], axis=0).astype(F32)
        k_own = jnp.where(lane_slot == slot, k2, 0.0)
        k_oth = pltpu.roll(k_own, HEAD_DIM, 1)
        k_own, k_oth = k_own.astype(BF16), k_oth.astype(BF16)
        k_zero = jnp.zeros_like(k_own)

        def k_rows(j):
            piece = k_own if j % HEADS_PER_VREG == slot else k_oth
            return jnp.concatenate([piece, k_zero] if j < HEADS_PER_VREG else [k_zero, piece], axis=1)

        k_bd = [k_rows(j) for j in range(HEADS_PER_DOT)]
        n_band = 2 * WINDOW
        k_band = jnp.concatenate([kb[:n_band] for kb in k_bd], axis=0)
        k_meta = jnp.concatenate([kb[n_band:] for kb in k_bd], axis=0)

        v2 = jnp.concatenate([kvp_ref[:, vcols], kvc_ref[:, vcols]], axis=0).astype(F32)
        vm2 = jnp.concatenate([kvm_ref[:, vcols].astype(F32), meta_pad], axis=0)
        rows = slice(slot * HEAD_DIM, (slot + 1) * HEAD_DIM)
        v_t = jnp.concatenate([v2.T[rows], vm2.T[rows]], axis=1).astype(BF16)

        for part in range(Q_PER_KV // HEADS_PER_DOT):
            h0 = kvh * Q_PER_KV + part * HEADS_PER_DOT
            qcols = slice(h0 * HEAD_DIM, (h0 + HEADS_PER_DOT) * HEAD_DIM)
            xq = q_ref[:, qcols]
            st = lax.dot_general(k_band, xq, _NT, preferred_element_type=F32)
            stm = lax.dot_general(k_meta, xq, _NT, preferred_element_type=F32)
            outs = []
            for j in range(HEADS_PER_DOT):
                s_prev = st[j * n_band:j * n_band + WINDOW] + prev_bias
                s_cur = st[j * n_band + WINDOW:(j + 1) * n_band]
                s = jnp.where(in_cur, s_cur, s_prev)
                sm = stm[j * N_META:(j + 1) * N_META]
                sink = sink_ref[h0 + j]
                mx = jnp.maximum(jnp.maximum(jnp.max(s, axis=0, keepdims=True),
                                             jnp.max(sm, axis=0, keepdims=True)), sink)
                p = jnp.exp(s - mx)
                pm = jnp.exp(sm - mx)
                denom = (jnp.sum(p, axis=0, keepdims=True) + jnp.sum(pm, axis=0, keepdims=True)
                         + jnp.exp(sink - mx))
                p_all = jnp.concatenate([jnp.where(in_cur, 0.0, p), jnp.where(in_cur, p, 0.0), pm, meta_pad],
                                        axis=0).astype(BF16)
                o_t = jnp.dot(v_t, p_all, preferred_element_type=F32)
                outs.append(o_t * (1.0 / denom))
            o_ref[:, qcols] = jnp.concatenate(outs, axis=0).T.astype(o_ref.dtype)


def _attention(q, kv, kv_meta, sinks, batch, seq):
    nb = seq // WINDOW
    return pl.pallas_call(
        _attn_body,
        grid=(batch, nb),
        in_specs=[pl.BlockSpec(memory_space=pltpu.SMEM),
                  pl.BlockSpec((WINDOW, ATTN_WIDTH), lambda b, i: (b * nb + i, 0)),
                  pl.BlockSpec((WINDOW, 2 * KV_WIDTH), lambda b, i: (b * nb + jnp.maximum(i - 1, 0), 0)),
                  pl.BlockSpec((WINDOW, 2 * KV_WIDTH), lambda b, i: (b * nb + i, 0)),
                  pl.BlockSpec((N_META, 2 * KV_WIDTH), lambda b, i: (0, 0))],
        out_specs=pl.BlockSpec((WINDOW, ATTN_WIDTH), lambda b, i: (b * nb + i, 0)),
        out_shape=jax.ShapeDtypeStruct((batch * seq, ATTN_WIDTH), BF16),
        compiler_params=_params(2),
        name="swa_attention",
    )(sinks, q, kv, kv, kv_meta)


POOL_HALO = 16


def _pool_body(up_ref, uc_ref, um_ref, w_ref, sc_ref, o_ref, ext_ref, *, bt):
    i = pl.program_id(1)
    ext_ref[0:POOL_HALO, :] = jnp.where(i == 0, um_ref[...], up_ref[...])
    ext_ref[POOL_HALO:, :] = uc_ref[...]
    for g, w in enumerate(POOL_WINDOWS):
        cs = slice(g * POOL_GROUP_WIDTH, (g + 1) * POOL_GROUP_WIDTH)
        cur = uc_ref[:, cs]
        win = cur
        for j in range(1, w):
            win = win + ext_ref[POOL_HALO - j:POOL_HALO - j + bt, cs]
        pooled = win / float(w) - cur
        mixed = jnp.dot(pooled.astype(BF16), w_ref[g], preferred_element_type=F32)
        o_ref[:, cs] = (mixed * sc_ref[:, cs]).astype(o_ref.dtype)


def _pool(u, u_meta, w_grp, scale, batch, seq, bt):
    nb = seq // bt
    halo_per_blk = bt // POOL_HALO
    return pl.pallas_call(
        functools.partial(_pool_body, bt=bt),
        grid=(batch, nb),
        in_specs=[pl.BlockSpec((POOL_HALO, POOL_WIDTH),
                               lambda b, i: (jnp.maximum((b * nb + i) * halo_per_blk - 1, 0), 0)),
                  pl.BlockSpec((bt, POOL_WIDTH), lambda b, i: (b * nb + i, 0)),
                  pl.BlockSpec((N_META, POOL_WIDTH), lambda b, i: (0, 0)),
                  pl.BlockSpec((len(POOL_WINDOWS), POOL_GROUP_WIDTH, POOL_GROUP_WIDTH), lambda b, i: (0, 0, 0)),
                  pl.BlockSpec((1, POOL_WIDTH), lambda b, i: (0, 0))],
        out_specs=pl.BlockSpec((bt, POOL_WIDTH), lambda b, i: (b * nb + i, 0)),
        out_shape=jax.ShapeDtypeStruct((batch * seq, POOL_WIDTH), BF16),
        scratch_shapes=[pltpu.VMEM((POOL_HALO + bt, POOL_WIDTH), F32)],
        compiler_params=_params(2),
        name="multiscale_pool",
    )(u, u, u_meta, w_grp, scale)


def _up_body(a_ref, p_ref, wa_ref, wp_ref, ga_ref, gb_ref, o_ref, wab_ref, wpb_ref):
    _cast_weights_once([wa_ref, wp_ref], [wab_ref, wpb_ref])
    for rs in _row_chunks(a_ref.shape[0]):
        a = jnp.dot(a_ref[rs, :], wab_ref[...], preferred_element_type=F32)
        p = jnp.dot(p_ref[rs, :], wpb_ref[...], preferred_element_type=F32)
        o_ref[rs, :] = (ga_ref[rs, :].astype(F32) * a + gb_ref[rs, :].astype(F32) * p).astype(o_ref.dtype)


def _up_merge(attn, pool, wa, wp, gates, bm, bn):
    m = attn.shape[0]
    n = wa.shape[2]
    nj = n // bn
    return pl.pallas_call(
        _up_body,
        grid=(nj, m // bm),
        in_specs=[pl.BlockSpec((bm, attn.shape[1]), lambda j, i: (i, 0)),
                  pl.BlockSpec((bm, pool.shape[1]), lambda j, i: (i, 0)),
                  pl.BlockSpec((None, wa.shape[1], bn), lambda j, i: (0, 0, j)),
                  pl.BlockSpec((None, wp.shape[1], bn), lambda j, i: (0, 0, j)),
                  pl.BlockSpec((bm, bn), lambda j, i: (i, j)),
                  pl.BlockSpec((bm, bn), lambda j, i: (i, j + nj))],
        out_specs=pl.BlockSpec((bm, bn), lambda j, i: (i, j)),
        out_shape=jax.ShapeDtypeStruct((m, n), BF16),
        scratch_shapes=[pltpu.VMEM((wa.shape[1], bn), BF16), pltpu.VMEM((wp.shape[1], bn), BF16)],
        compiler_params=_params(2),
        name="up_merge",
    )(attn, pool, wa, wp, gates, gates)


def _resid_mm_body(x_ref, w_ref, r_ref, mu_ref, rstd_ref, g_ref, b_ref, o_ref, *wb_refs):
    if wb_refs:
        _cast_weights_once([w_ref], wb_refs)
        w_ref = wb_refs[0]
    for rs in _row_chunks(x_ref.shape[0]):
        acc = jnp.dot(x_ref[rs, :], w_ref[...], preferred_element_type=F32)
        mu, rstd = mu_ref[rs, :], rstd_ref[rs, :]
        for c in range(acc.shape[1] // LANES):
            cs = slice(c * LANES, (c + 1) * LANES)
            h = _normalize(r_ref[rs, cs], mu, rstd, g_ref[:, cs], b_ref[:, cs])
            o_ref[rs, cs] = DN_ALPHA * h + acc[:, cs]


def _resid_matmul(x, w, r, mu, rstd, g, b, bm, bn, name):
    m, k = x.shape
    cast = w.dtype != BF16
    n = w.shape[-1]
    w_spec = (pl.BlockSpec((None, k, bn), lambda j, i: (0, 0, j)) if cast
              else pl.BlockSpec((k, bn), lambda j, i: (0, j)))
    tile = pl.BlockSpec((bm, bn), lambda j, i: (i, j))
    stat = pl.BlockSpec((bm, LANES), lambda j, i: (i, 0))
    vec = pl.BlockSpec((1, bn), lambda j, i: (0, j))
    return pl.pallas_call(
        _resid_mm_body,
        grid=(n // bn, m // bm),
        in_specs=[pl.BlockSpec((bm, k), lambda j, i: (i, 0)), w_spec, tile, stat, stat, vec, vec],
        out_specs=tile,
        out_shape=jax.ShapeDtypeStruct((m, n), F32),
        scratch_shapes=[pltpu.VMEM((k, bn), BF16)] if cast else [],
        compiler_params=_params(2),
        name=name,
    )(x, w, r, mu, rstd, g.reshape(1, n), b.reshape(1, n))


def _swiglu_body(x_ref, wg_ref, wu_ref, o_ref, wgb_ref, wub_ref):
    _cast_weights_once([wg_ref, wu_ref], [wgb_ref, wub_ref])
    for rs in _row_chunks(x_ref.shape[0]):
        x = x_ref[rs, :]
        g = jnp.dot(x, wgb_ref[...], preferred_element_type=F32)
        u = jnp.dot(x, wub_ref[...], preferred_element_type=F32)
        o_ref[rs, :] = (jax.nn.silu(g) * u).astype(o_ref.dtype)


def _swiglu_in(x, w, d_ff, bm, bn):
    m, k = x.shape
    nj = d_ff // bn
    return pl.pallas_call(
        _swiglu_body,
        grid=(nj, m // bm),
        in_specs=[pl.BlockSpec((bm, k), lambda j, i: (i, 0)),
                  pl.BlockSpec((None, k, bn), lambda j, i: (0, 0, j)),
                  pl.BlockSpec((None, k, bn), lambda j, i: (0, 0, j + nj))],
        out_specs=pl.BlockSpec((bm, bn), lambda j, i: (i, j)),
        out_shape=jax.ShapeDtypeStruct((m, d_ff), BF16),
        scratch_shapes=[pltpu.VMEM((k, bn), BF16), pltpu.VMEM((k, bn), BF16)],
        compiler_params=_params(2),
        name="swiglu_in",
    )(x, w, w)


def kernel(x, meta_tokens, ln_in_g, ln_in_b, w_in, b_gate, attn_sinks, w_attn_up, w_pool_grp, pool_scale,
           w_pool_up, w_out, ln1_g, ln1_b, w_ffn_in, w_ffn_down, ln2_g, ln2_b):
    batch, seq, d = x.shape
    m = batch * seq
    x2 = x.reshape(m, d)
    q0, k0, _, u0, g0 = IN_SPLITS

    tab_x = _rope_tables(jnp.arange(N_META, N_META + seq))
    tab_m = _rope_tables(jnp.arange(N_META))

    h0b, mu0, rstd0 = _layer_norm(x2, ln_in_g, ln_in_b, BF16, LN_ROW_TILE, True)
    (hmb,) = _layer_norm(meta_tokens, ln_in_g, ln_in_b, BF16, N_META, False)

    rope_q = functools.partial(_proj_rope_body, rope_cols=COL_TILE, scale=ATTN_SCALE)
    rope_kv = functools.partial(_proj_rope_body, rope_cols=KV_WIDTH, scale=1.0)
    q = _in_proj(rope_q, h0b, w_in, q0, ATTN_WIDTH, [tab_x], [_rope_spec(tab_x, ROW_TILE)], BF16, ROW_TILE, "proj_q")
    kv = _in_proj(rope_kv, h0b, w_in, k0, 2 * KV_WIDTH, [tab_x], [_rope_spec(tab_x, ROW_TILE)], BF16, ROW_TILE,
                  "proj_kv")
    kv_meta = _in_proj(rope_kv, hmb, w_in, k0, 2 * KV_WIDTH, [tab_m], [_rope_spec(tab_m, N_META)], BF16, N_META,
                       "proj_kv_meta")
    u = _in_proj(_proj_plain_body, h0b, w_in, u0, POOL_WIDTH, [], [], F32, ROW_TILE, "proj_u")
    u_meta = _in_proj(_proj_plain_body, hmb, w_in, u0, POOL_WIDTH, [], [], F32, N_META, "proj_u_meta")
    gates = _in_proj(_proj_gate_body, h0b, w_in, g0, 2 * d, [b_gate[0].reshape(1, 2 * d)],
                     [pl.BlockSpec((1, COL_TILE), lambda j, i: (0, j))], BF16, ROW_TILE, "proj_gate")

    attn = _attention(q, kv, kv_meta, attn_sinks[0], batch, seq)
    pool = _pool(u, u_meta, w_pool_grp[0].astype(BF16), pool_scale[0].reshape(1, POOL_WIDTH), batch, seq,
                 POOL_ROW_TILE)

    mixed = _up_merge(attn, pool, w_attn_up, w_pool_up, gates, ROW_TILE, COL_TILE)
    y1 = _resid_matmul(mixed, w_out, x2, mu0, rstd0, ln_in_g, ln_in_b, ROW_TILE, COL_TILE, "out_proj")
    h1b, mu1, rstd1 = _layer_norm(y1, ln1_g[0], ln1_b[0], BF16, LN_ROW_TILE, True)

    act = _swiglu_in(h1b, w_ffn_in, D_FF, ROW_TILE, FFN_COL_TILE)
    y2 = _resid_matmul(act, w_ffn_down[0].astype(BF16), y1, mu1, rstd1, ln1_g[0], ln1_b[0],
                       DOWN_ROW_TILE, COL_TILE, "ffn_down")
    (out,) = _layer_norm(y2, ln2_g[0], ln2_b[0], F32, LN_ROW_TILE, False)
    return out.reshape(batch, seq, d)
```

```python
import functools

import jax
import jax.numpy as jnp
from jax import lax
from jax.experimental import pallas as pl
from jax.experimental.pallas import tpu as pltpu

F32 = jnp.float32
BF16 = jnp.bfloat16

D_MODEL = 4096
N_META = 16
HEAD_DIM = 64
N_Q_HEADS = 32
N_KV_HEADS = 4
Q_PER_KV = N_Q_HEADS // N_KV_HEADS
ATTN_WIDTH = N_Q_HEADS * HEAD_DIM
KV_WIDTH = N_KV_HEADS * HEAD_DIM
WINDOW = 128
ATTN_SCALE = HEAD_DIM ** -0.5
ROPE_DIM = HEAD_DIM // 4
ROPE_HALF = ROPE_DIM // 2
ROPE_THETA = 500000.0
NEG_INF = -1e30
POOL_WINDOWS = (2, 4, 8, 16)
POOL_WIDTH = D_MODEL // 2
POOL_GROUP_WIDTH = POOL_WIDTH // len(POOL_WINDOWS)
IN_SPLITS = (0, ATTN_WIDTH, ATTN_WIDTH + KV_WIDTH, ATTN_WIDTH + 2 * KV_WIDTH,
             ATTN_WIDTH + 2 * KV_WIDTH + POOL_WIDTH)
D_FF = -(-8 * D_MODEL // 768) * 256
DEPTH = 1
DN_ALPHA = (2 * DEPTH) ** 0.25
LN_EPS = 1e-5

LANES = 128
VMEM_LIMIT = 56 * 1024 * 1024

ROW_TILE = 1024
COL_TILE = 512
FFN_COL_TILE = 256
FFN_ROW_TILE = 2048
DOWN_ROW_TILE = 512
LN_ROW_TILE = 512
POOL_ROW_TILE = 256


def _params(n_grid):
    return pltpu.CompilerParams(dimension_semantics=("arbitrary",) * n_grid, vmem_limit_bytes=VMEM_LIMIT)


MM_ROW_CHUNK = 256


def _row_chunks(bm):
    chunk = min(bm, MM_ROW_CHUNK)
    return [slice(r, r + chunk) for r in range(0, bm, chunk)]


def _cast_weights_once(w_refs, wb_refs):
    @pl.when(pl.program_id(1) == 0)
    def _():
        for w_ref, wb_ref in zip(w_refs, wb_refs):
            wb_ref[...] = w_ref[...].astype(BF16)


def _normalize(x, mu, rstd, g, b):
    return (x - mu) * rstd * g + b


def _ln_body(x_ref, g_ref, b_ref, y_ref, *stat_refs):
    x = x_ref[...]
    mu = jnp.mean(x, axis=-1, keepdims=True)
    xc = x - mu
    rstd = lax.rsqrt(jnp.mean(xc * xc, axis=-1, keepdims=True) + LN_EPS)
    y_ref[...] = _normalize(x, mu, rstd, g_ref[...], b_ref[...]).astype(y_ref.dtype)
    if stat_refs:
        mu_ref, rstd_ref = stat_refs
        mu_ref[...] = jnp.broadcast_to(mu, mu_ref.shape)
        rstd_ref[...] = jnp.broadcast_to(rstd, rstd_ref.shape)


def _layer_norm(x, g, b, out_dtype, bm, with_stats):
    m, d = x.shape
    row = pl.BlockSpec((bm, d), lambda i: (i, 0))
    vec = pl.BlockSpec((1, d), lambda i: (0, 0))
    stat = pl.BlockSpec((bm, LANES), lambda i: (i, 0))
    n_stats = 2 if with_stats else 0
    return pl.pallas_call(
        _ln_body,
        grid=(m // bm,),
        in_specs=[row, vec, vec],
        out_specs=[row] + [stat] * n_stats,
        out_shape=[jax.ShapeDtypeStruct((m, d), out_dtype)] + [jax.ShapeDtypeStruct((m, LANES), F32)] * n_stats,
        compiler_params=_params(1),
        name="layer_norm",
    )(x, g.reshape(1, d), b.reshape(1, d))


def _rope_tables(pos):
    inv_freq = ROPE_THETA ** (-jnp.arange(ROPE_HALF, dtype=F32) * 2.0 / ROPE_DIM)
    ang = pos.astype(F32)[:, None] * inv_freq[None, :]
    lane = jnp.arange(LANES) % HEAD_DIM
    cos = jnp.cos(ang)[:, lane % ROPE_HALF]
    sin = jnp.sin(ang)[:, lane % ROPE_HALF]
    c = jnp.where(lane < ROPE_DIM, cos, 1.0)
    s_lo = jnp.where(lane < ROPE_HALF, -sin, 0.0)
    s_hi = jnp.where((lane >= ROPE_HALF) & (lane < ROPE_DIM), sin, 0.0)
    return jnp.stack([c, s_lo, s_hi])


def _rope_chunk(x, tab_ref, rs):
    return (x * tab_ref[0, rs, :]
            + pltpu.roll(x, LANES - ROPE_HALF, 1) * tab_ref[1, rs, :]
            + pltpu.roll(x, ROPE_HALF, 1) * tab_ref[2, rs, :])


def _proj_rope_body(x_ref, w_ref, tab_ref, o_ref, wb_ref, *, rope_cols, scale):
    _cast_weights_once([w_ref], [wb_ref])
    for rs in _row_chunks(x_ref.shape[0]):
        acc = jnp.dot(x_ref[rs, :], wb_ref[...], preferred_element_type=F32)
        for c in range(acc.shape[1] // LANES):
            blk = acc[:, c * LANES:(c + 1) * LANES]
            if c * LANES < rope_cols:
                blk = _rope_chunk(blk, tab_ref, rs)
            if scale != 1.0:
                blk = blk * scale
            o_ref[rs, c * LANES:(c + 1) * LANES] = blk.astype(o_ref.dtype)


def _proj_plain_body(x_ref, w_ref, o_ref, wb_ref):
    _cast_weights_once([w_ref], [wb_ref])
    for rs in _row_chunks(x_ref.shape[0]):
        o_ref[rs, :] = jnp.dot(x_ref[rs, :], wb_ref[...], preferred_element_type=F32).astype(o_ref.dtype)


def _proj_gate_body(x_ref, w_ref, b_ref, o_ref, wb_ref):
    _cast_weights_once([w_ref], [wb_ref])
    for rs in _row_chunks(x_ref.shape[0]):
        acc = jnp.dot(x_ref[rs, :], wb_ref[...], preferred_element_type=F32)
        o_ref[rs, :] = jax.nn.sigmoid(acc + b_ref[...]).astype(o_ref.dtype)


def _in_proj(body, x, w_in, col0, n, extra, extra_specs, out_dtype, bm, name):
    m, k = x.shape
    bn = COL_TILE
    j0 = col0 // bn
    return pl.pallas_call(
        body,
        grid=(n // bn, m // bm),
        in_specs=[pl.BlockSpec((bm, k), lambda j, i: (i, 0)),
                  pl.BlockSpec((None, k, bn), lambda j, i: (0, 0, j0 + j))] + extra_specs,
        out_specs=pl.BlockSpec((bm, bn), lambda j, i: (i, j)),
        out_shape=jax.ShapeDtypeStruct((m, n), out_dtype),
        scratch_shapes=[pltpu.VMEM((k, bn), BF16)],
        compiler_params=_params(2),
        name=name,
    )(x, w_in, *extra)


def _rope_spec(tab, bm):
    t_blocks = tab.shape[1] // bm
    return pl.BlockSpec((3, bm, LANES), lambda j, i: (0, i % t_blocks, 0))


_NT = (((1,), (1,)), ((), ()))


HEADS_PER_DOT = 4
HEADS_PER_VREG = LANES // HEAD_DIM


def _attn_body(sink_ref, q_ref, kvp_ref, kvc_ref, kvm_ref, o_ref):
    i = pl.program_id(1)
    key = lax.broadcasted_iota(jnp.int32, (WINDOW, WINDOW), 0)
    qry = lax.broadcasted_iota(jnp.int32, (WINDOW, WINDOW), 1)
    in_cur = key <= qry
    prev_bias = jnp.where(i > 0, 0.0, NEG_INF).astype(F32)
    lane_slot = lax.broadcasted_iota(jnp.int32, (1, LANES), 1) // HEAD_DIM
    meta_pad = jnp.zeros((LANES - N_META, LANES), F32)

    for kvh in range(N_KV_HEADS):
        blk, slot = divmod(kvh, HEADS_PER_VREG)
        kcols = slice(blk * LANES, (blk + 1) * LANES)
        vcols = slice(KV_WIDTH + blk * LANES, KV_WIDTH + (blk + 1) * LANES)

        k2 = jnp.concatenate([kvp_ref[:, kcols], kvc_ref[:, kcols], kvm_ref[:, kcols]], axis=0).astype(F32)
        k_own = jnp.where(lane_slot == slot, k2, 0.0)
        k_oth = pltpu.roll(k_own, HEAD_DIM, 1)
        k_own, k_oth = k_own.astype(BF16), k_oth.astype(BF16)

        k_par = [jnp.concatenate([k_own if e == slot else k_oth] * HEADS_PER_VREG, axis=1)
                 for e in range(HEADS_PER_VREG)]
        n_band = 2 * WINDOW
        k_band = jnp.concatenate([kp[:n_band] for kp in k_par], axis=0)
        k_meta = jnp.concatenate([kp[n_band:] for kp in k_par], axis=0)

        v2 = jnp.concatenate([kvp_ref[:, vcols], kvc_ref[:, vcols]], axis=0).astype(F32)
        vm2 = jnp.concatenate([kvm_ref[:, vcols].astype(F32), meta_pad], axis=0)
        rows = slice(slot * HEAD_DIM, (slot + 1) * HEAD_DIM)
        v_t = jnp.concatenate([v2.T[rows], vm2.T[rows]], axis=1).astype(BF16)

        for part in range(Q_PER_KV // HEADS_PER_DOT):
            h0 = kvh * Q_PER_KV + part * HEADS_PER_DOT
            qcols = slice(h0 * HEAD_DIM, (h0 + HEADS_PER_DOT) * HEAD_DIM)
            xq = q_ref[:, qcols]
            q_zero = jnp.zeros((WINDOW, LANES), BF16)
            q_bd = jnp.concatenate([jnp.concatenate([xq[:, :LANES], q_zero], axis=1),
                                    jnp.concatenate([q_zero, xq[:, LANES:]], axis=1)], axis=0)
            st = lax.dot_general(k_band, q_bd, _NT, preferred_element_type=F32)
            stm = lax.dot_general(k_meta, q_bd, _NT, preferred_element_type=F32)
            outs = []
            for pair in range(HEADS_PER_DOT // HEADS_PER_VREG):
                qs = slice(pair * WINDOW, (pair + 1) * WINDOW)
                probs, inv = [], []
                for e in range(HEADS_PER_VREG):
                    s_prev = st[e * n_band:e * n_band + WINDOW, qs] + prev_bias
                    s_cur = st[e * n_band + WINDOW:(e + 1) * n_band, qs]
                    s = jnp.where(in_cur, s_cur, s_prev)
                    sm = stm[e * N_META:(e + 1) * N_META, qs]
                    sink = sink_ref[h0 + pair * HEADS_PER_VREG + e]
                    mx = jnp.maximum(jnp.maximum(jnp.max(s, axis=0, keepdims=True),
                                                 jnp.max(sm, axis=0, keepdims=True)), sink)
                    p = jnp.exp(s - mx)
                    pm = jnp.exp(sm - mx)
                    denom = (jnp.sum(p, axis=0, keepdims=True) + jnp.sum(pm, axis=0, keepdims=True)
                             + jnp.exp(sink - mx))
                    probs.append(jnp.concatenate(
                        [jnp.where(in_cur, 0.0, p), jnp.where(in_cur, p, 0.0), pm, meta_pad], axis=0))
                    inv.append(1.0 / denom)
                p_pair = jnp.concatenate(probs, axis=1).astype(BF16)
                o_pair = jnp.dot(v_t, p_pair, preferred_element_type=F32)
                outs += [o_pair[:, e * WINDOW:(e + 1) * WINDOW] * inv[e] for e in range(HEADS_PER_VREG)]
            o_ref[:, qcols] = jnp.concatenate(outs, axis=0).T.astype(o_ref.dtype)


def _attention(q, kv, kv_meta, sinks, batch, seq):
    nb = seq // WINDOW
    return pl.pallas_call(
        _attn_body,
        grid=(batch, nb),
        in_specs=[pl.BlockSpec(memory_space=pltpu.SMEM),
                  pl.BlockSpec((WINDOW, ATTN_WIDTH), lambda b, i: (b * nb + i, 0)),
                  pl.BlockSpec((WINDOW, 2 * KV_WIDTH), lambda b, i: (b * nb + jnp.maximum(i - 1, 0), 0)),
                  pl.BlockSpec((WINDOW, 2 * KV_WIDTH), lambda b, i: (b * nb + i, 0)),
                  pl.BlockSpec((N_META, 2 * KV_WIDTH), lambda b, i: (0, 0))],
        out_specs=pl.BlockSpec((WINDOW, ATTN_WIDTH), lambda b, i: (b * nb + i, 0)),
        out_shape=jax.ShapeDtypeStruct((batch * seq, ATTN_WIDTH), BF16),
        compiler_params=_params(2),
        name="swa_attention",
    )(sinks, q, kv, kv, kv_meta)


POOL_HALO = 16


def _pool_body(up_ref, uc_ref, um_ref, w_ref, sc_ref, o_ref, ext_ref, *, bt):
    i = pl.program_id(1)
    ext_ref[0:POOL_HALO, :] = jnp.where(i == 0, um_ref[...], up_ref[...])
    ext_ref[POOL_HALO:, :] = uc_ref[...]
    for g, w in enumerate(POOL_WINDOWS):
        cs = slice(g * POOL_GROUP_WIDTH, (g + 1) * POOL_GROUP_WIDTH)
        cur = uc_ref[:, cs]
        win = cur
        for j in range(1, w):
            win = win + ext_ref[POOL_HALO - j:POOL_HALO - j + bt, cs]
        pooled = win / float(w) - cur
        mixed = jnp.dot(pooled.astype(BF16), w_ref[g], preferred_element_type=F32)
        o_ref[:, cs] = (mixed * sc_ref[:, cs]).astype(o_ref.dtype)


def _pool(u, u_meta, w_grp, scale, batch, seq, bt):
    nb = seq // bt
    halo_per_blk = bt // POOL_HALO
    return pl.pallas_call(
        functools.partial(_pool_body, bt=bt),
        grid=(batch, nb),
        in_specs=[pl.BlockSpec((POOL_HALO, POOL_WIDTH),
                               lambda b, i: (jnp.maximum((b * nb + i) * halo_per_blk - 1, 0), 0)),
                  pl.BlockSpec((bt, POOL_WIDTH), lambda b, i: (b * nb + i, 0)),
                  pl.BlockSpec((N_META, POOL_WIDTH), lambda b, i: (0, 0)),
                  pl.BlockSpec((len(POOL_WINDOWS), POOL_GROUP_WIDTH, POOL_GROUP_WIDTH), lambda b, i: (0, 0, 0)),
                  pl.BlockSpec((1, POOL_WIDTH), lambda b, i: (0, 0))],
        out_specs=pl.BlockSpec((bt, POOL_WIDTH), lambda b, i: (b * nb + i, 0)),
        out_shape=jax.ShapeDtypeStruct((batch * seq, POOL_WIDTH), BF16),
        scratch_shapes=[pltpu.VMEM((POOL_HALO + bt, POOL_WIDTH), F32)],
        compiler_params=_params(2),
        name="multiscale_pool",
    )(u, u, u_meta, w_grp, scale)


def _up_body(a_ref, p_ref, wa_ref, wp_ref, ga_ref, gb_ref, o_ref, wab_ref, wpb_ref):
    _cast_weights_once([wa_ref, wp_ref], [wab_ref, wpb_ref])
    for rs in _row_chunks(a_ref.shape[0]):
        a = jnp.dot(a_ref[rs, :], wab_ref[...], preferred_element_type=F32)
        p = jnp.dot(p_ref[rs, :], wpb_ref[...], preferred_element_type=F32)
        o_ref[rs, :] = (ga_ref[rs, :].astype(F32) * a + gb_ref[rs, :].astype(F32) * p).astype(o_ref.dtype)


def _up_merge(attn, pool, wa, wp, gates, bm, bn):
    m = attn.shape[0]
    n = wa.shape[2]
    nj = n // bn
    return pl.pallas_call(
        _up_body,
        grid=(nj, m // bm),
        in_specs=[pl.BlockSpec((bm, attn.shape[1]), lambda j, i: (i, 0)),
                  pl.BlockSpec((bm, pool.shape[1]), lambda j, i: (i, 0)),
                  pl.BlockSpec((None, wa.shape[1], bn), lambda j, i: (0, 0, j)),
                  pl.BlockSpec((None, wp.shape[1], bn), lambda j, i: (0, 0, j)),
                  pl.BlockSpec((bm, bn), lambda j, i: (i, j)),
                  pl.BlockSpec((bm, bn), lambda j, i: (i, j + nj))],
        out_specs=pl.BlockSpec((bm, bn), lambda j, i: (i, j)),
        out_shape=jax.ShapeDtypeStruct((m, n), BF16),
        scratch_shapes=[pltpu.VMEM((wa.shape[1], bn), BF16), pltpu.VMEM((wp.shape[1], bn), BF16)],
        compiler_params=_params(2),
        name="up_merge",
    )(attn, pool, wa, wp, gates, gates)


def _resid_mm_body(x_ref, w_ref, r_ref, mu_ref, rstd_ref, g_ref, b_ref, o_ref, *wb_refs):
    if wb_refs:
        _cast_weights_once([w_ref], wb_refs)
        w_ref = wb_refs[0]
    for rs in _row_chunks(x_ref.shape[0]):
        acc = jnp.dot(x_ref[rs, :], w_ref[...], preferred_element_type=F32)
        mu, rstd = mu_ref[rs, :], rstd_ref[rs, :]
        for c in range(acc.shape[1] // LANES):
            cs = slice(c * LANES, (c + 1) * LANES)
            h = _normalize(r_ref[rs, cs], mu, rstd, g_ref[:, cs], b_ref[:, cs])
            o_ref[rs, cs] = DN_ALPHA * h + acc[:, cs]


def _resid_matmul(x, w, r, mu, rstd, g, b, bm, bn, name):
    m, k = x.shape
    cast = w.dtype != BF16
    n = w.shape[-1]
    w_spec = (pl.BlockSpec((None, k, bn), lambda j, i: (0, 0, j)) if cast
              else pl.BlockSpec((k, bn), lambda j, i: (0, j)))
    tile = pl.BlockSpec((bm, bn), lambda j, i: (i, j))
    stat = pl.BlockSpec((bm, LANES), lambda j, i: (i, 0))
    vec = pl.BlockSpec((1, bn), lambda j, i: (0, j))
    return pl.pallas_call(
        _resid_mm_body,
        grid=(n // bn, m // bm),
        in_specs=[pl.BlockSpec((bm, k), lambda j, i: (i, 0)), w_spec, tile, stat, stat, vec, vec],
        out_specs=tile,
        out_shape=jax.ShapeDtypeStruct((m, n), F32),
        scratch_shapes=[pltpu.VMEM((k, bn), BF16)] if cast else [],
        compiler_params=_params(2),
        name=name,
    )(x, w, r, mu, rstd, g.reshape(1, n), b.reshape(1, n))


def _swiglu_body(x_ref, wg_ref, wu_ref, o_ref, wgb_ref, wub_ref):
    _cast_weights_once([wg_ref, wu_ref], [wgb_ref, wub_ref])
    for rs in _row_chunks(x_ref.shape[0]):
        x = x_ref[rs, :]
        g = jnp.dot(x, wgb_ref[...], preferred_element_type=F32)
        u = jnp.dot(x, wub_ref[...], preferred_element_type=F32)
        o_ref[rs, :] = (jax.nn.silu(g) * u).astype(o_ref.dtype)


def _swiglu_in(x, w, d_ff, bm, bn):
    m, k = x.shape
    nj = d_ff // bn
    return pl.pallas_call(
        _swiglu_body,
        grid=(nj, m // bm),
        in_specs=[pl.BlockSpec((bm, k), lambda j, i: (i, 0)),
                  pl.BlockSpec((None, k, bn), lambda j, i: (0, 0, j)),
                  pl.BlockSpec((None, k, bn), lambda j, i: (0, 0, j + nj))],
        out_specs=pl.BlockSpec((bm, bn), lambda j, i: (i, j)),
        out_shape=jax.ShapeDtypeStruct((m, d_ff), BF16),
        scratch_shapes=[pltpu.VMEM((k, bn), BF16), pltpu.VMEM((k, bn), BF16)],
        compiler_params=_params(2),
        name="swiglu_in",
    )(x, w, w)


def kernel(x, meta_tokens, ln_in_g, ln_in_b, w_in, b_gate, attn_sinks, w_attn_up, w_pool_grp, pool_scale,
           w_pool_up, w_out, ln1_g, ln1_b, w_ffn_in, w_ffn_down, ln2_g, ln2_b):
    batch, seq, d = x.shape
    m = batch * seq
    x2 = x.reshape(m, d)
    q0, k0, _, u0, g0 = IN_SPLITS

    tab_x = _rope_tables(jnp.arange(N_META, N_META + seq))
    tab_m = _rope_tables(jnp.arange(N_META))

    h0b, mu0, rstd0 = _layer_norm(x2, ln_in_g, ln_in_b, BF16, LN_ROW_TILE, True)
    (hmb,) = _layer_norm(meta_tokens, ln_in_g, ln_in_b, BF16, N_META, False)

    rope_q = functools.partial(_proj_rope_body, rope_cols=COL_TILE, scale=ATTN_SCALE)
    rope_kv = functools.partial(_proj_rope_body, rope_cols=KV_WIDTH, scale=1.0)
    q = _in_proj(rope_q, h0b, w_in, q0, ATTN_WIDTH, [tab_x], [_rope_spec(tab_x, ROW_TILE)], BF16, ROW_TILE, "proj_q")
    kv = _in_proj(rope_kv, h0b, w_in, k0, 2 * KV_WIDTH, [tab_x], [_rope_spec(tab_x, ROW_TILE)], BF16, ROW_TILE,
                  "proj_kv")
    kv_meta = _in_proj(rope_kv, hmb, w_in, k0, 2 * KV_WIDTH, [tab_m], [_rope_spec(tab_m, N_META)], BF16, N_META,
                       "proj_kv_meta")
    u = _in_proj(_proj_plain_body, h0b, w_in, u0, POOL_WIDTH, [], [], F32, ROW_TILE, "proj_u")
    u_meta = _in_proj(_proj_plain_body, hmb, w_in, u0, POOL_WIDTH, [], [], F32, N_META, "proj_u_meta")
    gates = _in_proj(_proj_gate_body, h0b, w_in, g0, 2 * d, [b_gate[0].reshape(1, 2 * d)],
                     [pl.BlockSpec((1, COL_TILE), lambda j, i: (0, j))], BF16, ROW_TILE, "proj_gate")

    attn = _attention(q, kv, kv_meta, attn_sinks[0], batch, seq)
    pool = _pool(u, u_meta, w_pool_grp[0].astype(BF16), pool_scale[0].reshape(1, POOL_WIDTH), batch, seq,
                 POOL_ROW_TILE)

    mixed = _up_merge(attn, pool, w_attn_up, w_pool_up, gates, ROW_TILE, COL_TILE)
    y1 = _resid_matmul(mixed, w_out, x2, mu0, rstd0, ln_in_g, ln_in_b, ROW_TILE, COL_TILE, "out_proj")
    h1b, mu1, rstd1 = _layer_norm(y1, ln1_g[0], ln1_b[0], BF16, LN_ROW_TILE, True)

    act = _swiglu_in(h1b, w_ffn_in, D_FF, FFN_ROW_TILE, FFN_COL_TILE)
    y2 = _resid_matmul(act, w_ffn_down[0].astype(BF16), y1, mu1, rstd1, ln1_g[0], ln1_b[0],
                       DOWN_ROW_TILE, COL_TILE, "ffn_down")
    (out,) = _layer_norm(y2, ln2_g[0], ln2_b[0], F32, LN_ROW_TILE, False)
    return out.reshape(batch, seq, d)
```

```python
import functools

import jax
import jax.numpy as jnp
from jax import lax
from jax.experimental import pallas as pl
from jax.experimental.pallas import tpu as pltpu

F32 = jnp.float32
BF16 = jnp.bfloat16

D_MODEL = 4096
N_META = 16
HEAD_DIM = 64
N_Q_HEADS = 32
N_KV_HEADS = 4
Q_PER_KV = N_Q_HEADS // N_KV_HEADS
ATTN_WIDTH = N_Q_HEADS * HEAD_DIM
KV_WIDTH = N_KV_HEADS * HEAD_DIM
WINDOW = 128
ATTN_SCALE = HEAD_DIM ** -0.5
ROPE_DIM = HEAD_DIM // 4
ROPE_HALF = ROPE_DIM // 2
ROPE_THETA = 500000.0
NEG_INF = -1e30
POOL_WINDOWS = (2, 4, 8, 16)
POOL_WIDTH = D_MODEL // 2
POOL_GROUP_WIDTH = POOL_WIDTH // len(POOL_WINDOWS)
IN_SPLITS = (0, ATTN_WIDTH, ATTN_WIDTH + KV_WIDTH, ATTN_WIDTH + 2 * KV_WIDTH,
             ATTN_WIDTH + 2 * KV_WIDTH + POOL_WIDTH)
D_FF = -(-8 * D_MODEL // 768) * 256
DEPTH = 1
DN_ALPHA = (2 * DEPTH) ** 0.25
LN_EPS = 1e-5

LANES = 128
VMEM_LIMIT = 56 * 1024 * 1024

ROW_TILE = 1024
COL_TILE = 1024
STAGE_COLS = 512
FFN_COL_TILE = 256
FFN_ROW_TILE = 2048
DOWN_ROW_TILE = 512
DOWN_COL_TILE = 512
LN_ROW_TILE = 512
POOL_ROW_TILE = 256


def _params(n_grid):
    return pltpu.CompilerParams(dimension_semantics=("arbitrary",) * n_grid, vmem_limit_bytes=VMEM_LIMIT)


MM_ROW_CHUNK = 256


def _row_chunks(bm):
    chunk = min(bm, MM_ROW_CHUNK)
    return [slice(r, r + chunk) for r in range(0, bm, chunk)]


def _cast_weights_once(w_refs, wb_refs):
    @pl.when(pl.program_id(1) == 0)
    def _():
        for w_ref, wb_ref in zip(w_refs, wb_refs):
            wb_ref[...] = w_ref[...].astype(BF16)


def _tile_row(j, i):
    return jnp.where(j > 0, i, 0)


def _tile_col(j, i):
    return jnp.maximum(j - 1, 0)


def _staged_body(*refs, n_x, n_w, sub, compute):
    x_refs = refs[:n_x]
    w_refs = refs[n_x:n_x + n_w * sub]
    extra_refs = refs[n_x + n_w * sub:len(refs) - n_w - 1]
    o_ref = refs[len(refs) - n_w - 1]
    wb_refs = refs[len(refs) - n_w:]
    j, i = pl.program_id(0), pl.program_id(1)

    def stage():
        for t, wb_ref in enumerate(wb_refs):
            for h in range(sub):
                w_ref = w_refs[t * sub + h]
                kc = w_ref.shape[0]
                rows = pl.ds(pl.multiple_of(i * kc, kc), kc)
                wb_ref[j % 2, rows, h * STAGE_COLS:(h + 1) * STAGE_COLS] = w_ref[...].astype(BF16)

    @pl.when(j == 0)
    def _():
        stage()

    @pl.when(j > 0)
    def _():
        stage()
        compute(x_refs, [wb_ref.at[(j + 1) % 2] for wb_ref in wb_refs], extra_refs, o_ref)


def _staged_matmul(compute, xs, ws, extras, extra_specs, out_dtype, n, bm, bn, name):
    m = xs[0].shape[0]
    ni, nj = m // bm, n // bn
    sub = bn // STAGE_COLS
    in_specs = [pl.BlockSpec((bm, x.shape[1]), lambda j, i: (_tile_row(j, i), 0)) for x in xs]
    operands = list(xs)
    scratch = []
    for w, col0 in ws:
        k = w.shape[1]
        for h in range(sub):
            first = col0 // STAGE_COLS + h
            in_specs.append(pl.BlockSpec(
                (None, k // ni, STAGE_COLS),
                lambda j, i, first=first: (0, i, first + sub * jnp.minimum(j, nj - 1))))
            operands.append(w)
        scratch.append(pltpu.VMEM((2, k, bn), BF16))
    return pl.pallas_call(
        functools.partial(_staged_body, n_x=len(xs), n_w=len(ws), sub=sub, compute=compute),
        grid=(nj + 1, ni),
        in_specs=in_specs + extra_specs,
        out_specs=pl.BlockSpec((bm, bn), lambda j, i: (_tile_row(j, i), _tile_col(j, i))),
        out_shape=jax.ShapeDtypeStruct((m, n), out_dtype),
        scratch_shapes=scratch,
        compiler_params=_params(2),
        name=name,
    )(*operands, *extras)


def _normalize(x, mu, rstd, g, b):
    return (x - mu) * rstd * g + b


def _ln_body(x_ref, g_ref, b_ref, y_ref, *stat_refs):
    x = x_ref[...]
    mu = jnp.mean(x, axis=-1, keepdims=True)
    xc = x - mu
    rstd = lax.rsqrt(jnp.mean(xc * xc, axis=-1, keepdims=True) + LN_EPS)
    y_ref[...] = _normalize(x, mu, rstd, g_ref[...], b_ref[...]).astype(y_ref.dtype)
    if stat_refs:
        mu_ref, rstd_ref = stat_refs
        mu_ref[...] = jnp.broadcast_to(mu, mu_ref.shape)
        rstd_ref[...] = jnp.broadcast_to(rstd, rstd_ref.shape)


def _layer_norm(x, g, b, out_dtype, bm, with_stats):
    m, d = x.shape
    row = pl.BlockSpec((bm, d), lambda i: (i, 0))
    vec = pl.BlockSpec((1, d), lambda i: (0, 0))
    stat = pl.BlockSpec((bm, LANES), lambda i: (i, 0))
    n_stats = 2 if with_stats else 0
    return pl.pallas_call(
        _ln_body,
        grid=(m // bm,),
        in_specs=[row, vec, vec],
        out_specs=[row] + [stat] * n_stats,
        out_shape=[jax.ShapeDtypeStruct((m, d), out_dtype)] + [jax.ShapeDtypeStruct((m, LANES), F32)] * n_stats,
        compiler_params=_params(1),
        name="layer_norm",
    )(x, g.reshape(1, d), b.reshape(1, d))


def _rope_tables(pos):
    inv_freq = ROPE_THETA ** (-jnp.arange(ROPE_HALF, dtype=F32) * 2.0 / ROPE_DIM)
    ang = pos.astype(F32)[:, None] * inv_freq[None, :]
    lane = jnp.arange(LANES) % HEAD_DIM
    cos = jnp.cos(ang)[:, lane % ROPE_HALF]
    sin = jnp.sin(ang)[:, lane % ROPE_HALF]
    c = jnp.where(lane < ROPE_DIM, cos, 1.0)
    s_lo = jnp.where(lane < ROPE_HALF, -sin, 0.0)
    s_hi = jnp.where((lane >= ROPE_HALF) & (lane < ROPE_DIM), sin, 0.0)
    return jnp.stack([c, s_lo, s_hi])


def _rope_chunk(x, tab_ref, rs):
    return (x * tab_ref[0, rs, :]
            + pltpu.roll(x, LANES - ROPE_HALF, 1) * tab_ref[1, rs, :]
            + pltpu.roll(x, ROPE_HALF, 1) * tab_ref[2, rs, :])


def _rope_compute(x_refs, w_refs, extra_refs, o_ref, *, rope_cols, scale):
    (x_ref,), (w_ref,), (tab_ref,) = x_refs, w_refs, extra_refs
    for rs in _row_chunks(x_ref.shape[0]):
        acc = jnp.dot(x_ref[rs, :], w_ref[...], preferred_element_type=F32)
        for c in range(acc.shape[1] // LANES):
            blk = acc[:, c * LANES:(c + 1) * LANES]
            if c * LANES < rope_cols:
                blk = _rope_chunk(blk, tab_ref, rs)
            if scale != 1.0:
                blk = blk * scale
            o_ref[rs, c * LANES:(c + 1) * LANES] = blk.astype(o_ref.dtype)


def _plain_compute(x_refs, w_refs, extra_refs, o_ref):
    (x_ref,), (w_ref,) = x_refs, w_refs
    for rs in _row_chunks(x_ref.shape[0]):
        o_ref[rs, :] = jnp.dot(x_ref[rs, :], w_ref[...], preferred_element_type=F32).astype(o_ref.dtype)


def _gate_compute(x_refs, w_refs, extra_refs, o_ref):
    (x_ref,), (w_ref,), (b_ref,) = x_refs, w_refs, extra_refs
    for rs in _row_chunks(x_ref.shape[0]):
        acc = jnp.dot(x_ref[rs, :], w_ref[...], preferred_element_type=F32)
        o_ref[rs, :] = jax.nn.sigmoid(acc + b_ref[...]).astype(o_ref.dtype)


def _rope_spec(tab, bm):
    t_blocks = tab.shape[1] // bm
    return pl.BlockSpec((3, bm, LANES), lambda j, i: (0, _tile_row(j, i) % t_blocks, 0))


_NT = (((1,), (1,)), ((), ()))


HEADS_PER_DOT = 4
HEADS_PER_VREG = LANES // HEAD_DIM


def _attn_body(sink_ref, q_ref, kvp_ref, kvc_ref, kvm_ref, o_ref):
    i = pl.program_id(1)
    key = lax.broadcasted_iota(jnp.int32, (WINDOW, WINDOW), 0)
    qry = lax.broadcasted_iota(jnp.int32, (WINDOW, WINDOW), 1)
    in_cur = key <= qry
    prev_bias = jnp.where(i > 0, 0.0, NEG_INF).astype(F32)
    lane_slot = lax.broadcasted_iota(jnp.int32, (1, LANES), 1) // HEAD_DIM
    meta_pad = jnp.zeros((LANES - N_META, LANES), F32)

    for kvh in range(N_KV_HEADS):
        blk, slot = divmod(kvh, HEADS_PER_VREG)
        kcols = slice(blk * LANES, (blk + 1) * LANES)
        vcols = slice(KV_WIDTH + blk * LANES, KV_WIDTH + (blk + 1) * LANES)

        k2 = jnp.concatenate([kvp_ref[:, kcols], kvc_ref[:, kcols], kvm_ref[:, kcols]], axis=0).astype(F32)
        k_own = jnp.where(lane_slot == slot, k2, 0.0)
        k_oth = pltpu.roll(k_own, HEAD_DIM, 1)
        k_own, k_oth = k_own.astype(BF16), k_oth.astype(BF16)

        k_par = [jnp.concatenate([k_own if e == slot else k_oth] * HEADS_PER_VREG, axis=1)
                 for e in range(HEADS_PER_VREG)]
        n_band = 2 * WINDOW
        k_band = jnp.concatenate([kp[:n_band] for kp in k_par], axis=0)
        k_meta = jnp.concatenate([kp[n_band:] for kp in k_par], axis=0)

        v2 = jnp.concatenate([kvp_ref[:, vcols], kvc_ref[:, vcols]], axis=0).astype(F32)
        vm2 = jnp.concatenate([kvm_ref[:, vcols].astype(F32), meta_pad], axis=0)
        rows = slice(slot * HEAD_DIM, (slot + 1) * HEAD_DIM)
        v_t = jnp.concatenate([v2.T[rows], vm2.T[rows]], axis=1).astype(BF16)

        for part in range(Q_PER_KV // HEADS_PER_DOT):
            h0 = kvh * Q_PER_KV + part * HEADS_PER_DOT
            qcols = slice(h0 * HEAD_DIM, (h0 + HEADS_PER_DOT) * HEAD_DIM)
            xq = q_ref[:, qcols]
            q_zero = jnp.zeros((WINDOW, LANES), BF16)
            q_bd = jnp.concatenate([jnp.concatenate([xq[:, :LANES], q_zero], axis=1),
                                    jnp.concatenate([q_zero, xq[:, LANES:]], axis=1)], axis=0)
            st = lax.dot_general(k_band, q_bd, _NT, preferred_element_type=F32)
            stm = lax.dot_general(k_meta, q_bd, _NT, preferred_element_type=F32)
            outs = []
            for pair in range(HEADS_PER_DOT // HEADS_PER_VREG):
                qs = slice(pair * WINDOW, (pair + 1) * WINDOW)
                probs, inv = [], []
                for e in range(HEADS_PER_VREG):
                    s_prev = st[e * n_band:e * n_band + WINDOW, qs] + prev_bias
                    s_cur = st[e * n_band + WINDOW:(e + 1) * n_band, qs]
                    s = jnp.where(in_cur, s_cur, s_prev)
                    sm = stm[e * N_META:(e + 1) * N_META, qs]
                    sink = sink_ref[h0 + pair * HEADS_PER_VREG + e]
                    mx = jnp.maximum(jnp.maximum(jnp.max(s, axis=0, keepdims=True),
                                                 jnp.max(sm, axis=0, keepdims=True)), sink)
                    p = jnp.exp(s - mx)
                    pm = jnp.exp(sm - mx)
                    denom = (jnp.sum(p, axis=0, keepdims=True) + jnp.sum(pm, axis=0, keepdims=True)
                             + jnp.exp(sink - mx))
                    probs.append(jnp.concatenate(
                        [jnp.where(in_cur, 0.0, p), jnp.where(in_cur, p, 0.0), pm, meta_pad], axis=0))
                    inv.append(1.0 / denom)
                p_pair = jnp.concatenate(probs, axis=1).astype(BF16)
                o_pair = jnp.dot(v_t, p_pair, preferred_element_type=F32)
                outs += [o_pair[:, e * WINDOW:(e + 1) * WINDOW] * inv[e] for e in range(HEADS_PER_VREG)]
            o_ref[:, qcols] = jnp.concatenate(outs, axis=0).T.astype(o_ref.dtype)


def _attention(q, kv, kv_meta, sinks, batch, seq):
    nb = seq // WINDOW
    return pl.pallas_call(
        _attn_body,
        grid=(batch, nb),
        in_specs=[pl.BlockSpec(memory_space=pltpu.SMEM),
                  pl.BlockSpec((WINDOW, ATTN_WIDTH), lambda b, i: (b * nb + i, 0)),
                  pl.BlockSpec((WINDOW, 2 * KV_WIDTH), lambda b, i: (b * nb + jnp.maximum(i - 1, 0), 0)),
                  pl.BlockSpec((WINDOW, 2 * KV_WIDTH), lambda b, i: (b * nb + i, 0)),
                  pl.BlockSpec((N_META, 2 * KV_WIDTH), lambda b, i: (0, 0))],
        out_specs=pl.BlockSpec((WINDOW, ATTN_WIDTH), lambda b, i: (b * nb + i, 0)),
        out_shape=jax.ShapeDtypeStruct((batch * seq, ATTN_WIDTH), BF16),
        compiler_params=_params(2),
        name="swa_attention",
    )(sinks, q, kv, kv, kv_meta)


POOL_HALO = 16


def _pool_body(up_ref, uc_ref, um_ref, w_ref, sc_ref, o_ref, ext_ref, *, bt):
    i = pl.program_id(1)
    ext_ref[0:POOL_HALO, :] = jnp.where(i == 0, um_ref[...], up_ref[...])
    ext_ref[POOL_HALO:, :] = uc_ref[...]
    for g, w in enumerate(POOL_WINDOWS):
        cs = slice(g * POOL_GROUP_WIDTH, (g + 1) * POOL_GROUP_WIDTH)
        cur = uc_ref[:, cs]
        win = cur
        for j in range(1, w):
            win = win + ext_ref[POOL_HALO - j:POOL_HALO - j + bt, cs]
        pooled = win / float(w) - cur
        mixed = jnp.dot(pooled.astype(BF16), w_ref[g], preferred_element_type=F32)
        o_ref[:, cs] = (mixed * sc_ref[:, cs]).astype(o_ref.dtype)


def _pool(u, u_meta, w_grp, scale, batch, seq, bt):
    nb = seq // bt
    halo_per_blk = bt // POOL_HALO
    return pl.pallas_call(
        functools.partial(_pool_body, bt=bt),
        grid=(batch, nb),
        in_specs=[pl.BlockSpec((POOL_HALO, POOL_WIDTH),
                               lambda b, i: (jnp.maximum((b * nb + i) * halo_per_blk - 1, 0), 0)),
                  pl.BlockSpec((bt, POOL_WIDTH), lambda b, i: (b * nb + i, 0)),
                  pl.BlockSpec((N_META, POOL_WIDTH), lambda b, i: (0, 0)),
                  pl.BlockSpec((len(POOL_WINDOWS), POOL_GROUP_WIDTH, POOL_GROUP_WIDTH), lambda b, i: (0, 0, 0)),
                  pl.BlockSpec((1, POOL_WIDTH), lambda b, i: (0, 0))],
        out_specs=pl.BlockSpec((bt, POOL_WIDTH), lambda b, i: (b * nb + i, 0)),
        out_shape=jax.ShapeDtypeStruct((batch * seq, POOL_WIDTH), BF16),
        scratch_shapes=[pltpu.VMEM((POOL_HALO + bt, POOL_WIDTH), F32)],
        compiler_params=_params(2),
        name="multiscale_pool",
    )(u, u, u_meta, w_grp, scale)


def _up_compute(x_refs, w_refs, extra_refs, o_ref):
    (a_ref, p_ref), (wa_ref, wp_ref), (ga_ref, gb_ref) = x_refs, w_refs, extra_refs
    for rs in _row_chunks(a_ref.shape[0]):
        a = jnp.dot(a_ref[rs, :], wa_ref[...], preferred_element_type=F32)
        p = jnp.dot(p_ref[rs, :], wp_ref[...], preferred_element_type=F32)
        o_ref[rs, :] = (ga_ref[rs, :].astype(F32) * a + gb_ref[rs, :].astype(F32) * p).astype(o_ref.dtype)


def _resid_compute(x_refs, w_refs, extra_refs, o_ref):
    (x_ref,), (w_ref,), (r_ref, mu_ref, rstd_ref, g_ref, b_ref) = x_refs, w_refs, extra_refs
    for rs in _row_chunks(x_ref.shape[0]):
        acc = jnp.dot(x_ref[rs, :], w_ref[...], preferred_element_type=F32)
        mu, rstd = mu_ref[rs, :], rstd_ref[rs, :]
        for c in range(acc.shape[1] // LANES):
            cs = slice(c * LANES, (c + 1) * LANES)
            h = _normalize(r_ref[rs, cs], mu, rstd, g_ref[:, cs], b_ref[:, cs])
            o_ref[rs, cs] = DN_ALPHA * h + acc[:, cs]


def _resid_specs(bm, bn, row, col):
    return [pl.BlockSpec((bm, bn), lambda j, i: (row(j, i), col(j, i))),
            pl.BlockSpec((bm, LANES), lambda j, i: (row(j, i), 0)),
            pl.BlockSpec((bm, LANES), lambda j, i: (row(j, i), 0)),
            pl.BlockSpec((1, bn), lambda j, i: (0, col(j, i))),
            pl.BlockSpec((1, bn), lambda j, i: (0, col(j, i)))]


def _down_body(x_ref, w_ref, r_ref, mu_ref, rstd_ref, g_ref, b_ref, o_ref):
    _resid_compute([x_ref], [w_ref], [r_ref, mu_ref, rstd_ref, g_ref, b_ref], o_ref)


def _ffn_down(x, w, r, mu, rstd, g, b, bm, bn):
    m, k = x.shape
    n = w.shape[1]
    return pl.pallas_call(
        _down_body,
        grid=(n // bn, m // bm),
        in_specs=[pl.BlockSpec((bm, k), lambda j, i: (i, 0)), pl.BlockSpec((k, bn), lambda j, i: (0, j))]
        + _resid_specs(bm, bn, lambda j, i: i, lambda j, i: j),
        out_specs=pl.BlockSpec((bm, bn), lambda j, i: (i, j)),
        out_shape=jax.ShapeDtypeStruct((m, n), F32),
        compiler_params=_params(2),
        name="ffn_down",
    )(x, w, r, mu, rstd, g.reshape(1, n), b.reshape(1, n))


def _swiglu_body(x_ref, wg_ref, wu_ref, o_ref, wgb_ref, wub_ref):
    _cast_weights_once([wg_ref, wu_ref], [wgb_ref, wub_ref])
    for rs in _row_chunks(x_ref.shape[0]):
        x = x_ref[rs, :]
        g = jnp.dot(x, wgb_ref[...], preferred_element_type=F32)
        u = jnp.dot(x, wub_ref[...], preferred_element_type=F32)
        o_ref[rs, :] = (jax.nn.silu(g) * u).astype(o_ref.dtype)


def _swiglu_in(x, w, d_ff, bm, bn):
    m, k = x.shape
    nj = d_ff // bn
    return pl.pallas_call(
        _swiglu_body,
        grid=(nj, m // bm),
        in_specs=[pl.BlockSpec((bm, k), lambda j, i: (i, 0)),
                  pl.BlockSpec((None, k, bn), lambda j, i: (0, 0, j)),
                  pl.BlockSpec((None, k, bn), lambda j, i: (0, 0, j + nj))],
        out_specs=pl.BlockSpec((bm, bn), lambda j, i: (i, j)),
        out_shape=jax.ShapeDtypeStruct((m, d_ff), BF16),
        scratch_shapes=[pltpu.VMEM((k, bn), BF16), pltpu.VMEM((k, bn), BF16)],
        compiler_params=_params(2),
        name="swiglu_in",
    )(x, w, w)


def kernel(x, meta_tokens, ln_in_g, ln_in_b, w_in, b_gate, attn_sinks, w_attn_up, w_pool_grp, pool_scale,
           w_pool_up, w_out, ln1_g, ln1_b, w_ffn_in, w_ffn_down, ln2_g, ln2_b):
    batch, seq, d = x.shape
    m = batch * seq
    x2 = x.reshape(m, d)
    q0, k0, _, u0, g0 = IN_SPLITS

    tab_x = _rope_tables(jnp.arange(N_META, N_META + seq))
    tab_m = _rope_tables(jnp.arange(N_META))

    h0b, mu0, rstd0 = _layer_norm(x2, ln_in_g, ln_in_b, BF16, LN_ROW_TILE, True)
    (hmb,) = _layer_norm(meta_tokens, ln_in_g, ln_in_b, BF16, N_META, False)

    rope_q = functools.partial(_rope_compute, rope_cols=ATTN_WIDTH, scale=ATTN_SCALE)
    rope_kv = functools.partial(_rope_compute, rope_cols=KV_WIDTH, scale=1.0)
    wide, narrow = COL_TILE, STAGE_COLS
    q = _staged_matmul(rope_q, [h0b], [(w_in, q0)], [tab_x], [_rope_spec(tab_x, ROW_TILE)], BF16, ATTN_WIDTH,
                       ROW_TILE, wide, "proj_q")
    kv = _staged_matmul(rope_kv, [h0b], [(w_in, k0)], [tab_x], [_rope_spec(tab_x, ROW_TILE)], BF16, 2 * KV_WIDTH,
                        ROW_TILE, narrow, "proj_kv")
    kv_meta = _staged_matmul(rope_kv, [hmb], [(w_in, k0)], [tab_m], [_rope_spec(tab_m, N_META)], BF16,
                             2 * KV_WIDTH, N_META, narrow, "proj_kv_meta")
    u = _staged_matmul(_plain_compute, [h0b], [(w_in, u0)], [], [], F32, POOL_WIDTH, ROW_TILE, wide, "proj_u")
    u_meta = _staged_matmul(_plain_compute, [hmb], [(w_in, u0)], [], [], F32, POOL_WIDTH, N_META, narrow,
                            "proj_u_meta")
    gates = _staged_matmul(_gate_compute, [h0b], [(w_in, g0)], [b_gate[0].reshape(1, 2 * d)],
                           [pl.BlockSpec((1, wide), lambda j, i: (0, _tile_col(j, i)))], BF16, 2 * d,
                           ROW_TILE, wide, "proj_gate")

    attn = _attention(q, kv, kv_meta, attn_sinks[0], batch, seq)
    pool = _pool(u, u_meta, w_pool_grp[0].astype(BF16), pool_scale[0].reshape(1, POOL_WIDTH), batch, seq,
                 POOL_ROW_TILE)

    gate_b0 = d // wide
    mixed = _staged_matmul(
        _up_compute, [attn, pool], [(w_attn_up, 0), (w_pool_up, 0)], [gates, gates],
        [pl.BlockSpec((ROW_TILE, wide), lambda j, i: (_tile_row(j, i), _tile_col(j, i))),
         pl.BlockSpec((ROW_TILE, wide), lambda j, i: (_tile_row(j, i), _tile_col(j, i) + gate_b0))],
        BF16, d, ROW_TILE, wide, "up_merge")
    y1 = _staged_matmul(
        _resid_compute, [mixed], [(w_out, 0)], [x2, mu0, rstd0, ln_in_g.reshape(1, d), ln_in_b.reshape(1, d)],
        _resid_specs(ROW_TILE, wide, _tile_row, _tile_col), F32, d, ROW_TILE, wide, "out_proj")
    h1b, mu1, rstd1 = _layer_norm(y1, ln1_g[0], ln1_b[0], BF16, LN_ROW_TILE, True)

    act = _swiglu_in(h1b, w_ffn_in, D_FF, FFN_ROW_TILE, FFN_COL_TILE)
    y2 = _ffn_down(act, w_ffn_down[0].astype(BF16), y1, mu1, rstd1, ln1_g[0], ln1_b[0],
                   DOWN_ROW_TILE, DOWN_COL_TILE)
    (out,) = _layer_norm(y2, ln2_g[0], ln2_b[0], F32, LN_ROW_TILE, False)
    return out.reshape(batch, seq, d)
```

```python
import functools

import jax
import jax.numpy as jnp
from jax import lax
from jax.experimental import pallas as pl
from jax.experimental.pallas import tpu as pltpu

F32 = jnp.float32
BF16 = jnp.bfloat16

D_MODEL = 4096
N_META = 16
HEAD_DIM = 64
N_Q_HEADS = 32
N_KV_HEADS = 4
Q_PER_KV = N_Q_HEADS // N_KV_HEADS
ATTN_WIDTH = N_Q_HEADS * HEAD_DIM
KV_WIDTH = N_KV_HEADS * HEAD_DIM
WINDOW = 128
ATTN_SCALE = HEAD_DIM ** -0.5
ROPE_DIM = HEAD_DIM // 4
ROPE_HALF = ROPE_DIM // 2
ROPE_THETA = 500000.0
NEG_INF = -1e30
POOL_WINDOWS = (2, 4, 8, 16)
POOL_WIDTH = D_MODEL // 2
POOL_GROUP_WIDTH = POOL_WIDTH // len(POOL_WINDOWS)
IN_SPLITS = (0, ATTN_WIDTH, ATTN_WIDTH + KV_WIDTH, ATTN_WIDTH + 2 * KV_WIDTH,
             ATTN_WIDTH + 2 * KV_WIDTH + POOL_WIDTH)
D_FF = -(-8 * D_MODEL // 768) * 256
DEPTH = 1
DN_ALPHA = (2 * DEPTH) ** 0.25
LN_EPS = 1e-5

LANES = 128
VMEM_LIMIT = 56 * 1024 * 1024

ROW_TILE = 1024
COL_TILE = 1024
STAGE_COLS = 512
MAX_STAGE_CHUNKS = 16
FFN_COL_TILE = 256
FFN_ROW_TILE = 2048
DOWN_ROW_TILE = 512
DOWN_COL_TILE = 512
LN_ROW_TILE = 512
POOL_ROW_TILE = 512


def _params(n_grid):
    return pltpu.CompilerParams(dimension_semantics=("arbitrary",) * n_grid, vmem_limit_bytes=VMEM_LIMIT)


MM_ROW_CHUNK = 256


def _row_chunks(bm):
    chunk = min(bm, MM_ROW_CHUNK)
    return [slice(r, r + chunk) for r in range(0, bm, chunk)]


def _tile_row(j, i):
    return jnp.where(j > 0, i, 0)


def _tile_col(j, i):
    return jnp.maximum(j - 1, 0)


def _staged_body(*refs, n_x, n_w, sub, steps_per_chunk, compute):
    x_refs = refs[:n_x]
    w_refs = refs[n_x:n_x + n_w * sub]
    extra_refs = refs[n_x + n_w * sub:len(refs) - n_w - 1]
    o_ref = refs[len(refs) - n_w - 1]
    wb_refs = refs[len(refs) - n_w:]
    j, i = pl.program_id(0), pl.program_id(1)

    def stage():
        for t, wb_ref in enumerate(wb_refs):
            for h in range(sub):
                w_ref = w_refs[t * sub + h]
                kc = w_ref.shape[0]
                rows = pl.ds(pl.multiple_of((i // steps_per_chunk) * kc, kc), kc)
                sw = w_ref.shape[1]
                wb_ref[j % 2, rows, h * sw:(h + 1) * sw] = w_ref[...].astype(BF16)

    @pl.when(j == 0)
    def _():
        stage()

    @pl.when(j > 0)
    def _():
        stage()
        compute(x_refs, [wb_ref.at[(j + 1) % 2] for wb_ref in wb_refs], extra_refs, o_ref)


def _staged_matmul(compute, xs, ws, extras, extra_specs, out_dtype, n, bm, bn, name):
    m = xs[0].shape[0]
    ni, nj = m // bm, n // bn
    sw = min(bn, STAGE_COLS)
    sub = bn // sw
    n_chunks = min(ni, MAX_STAGE_CHUNKS)
    steps_per_chunk = ni // n_chunks
    in_specs = [pl.BlockSpec((bm, x.shape[1]), lambda j, i: (_tile_row(j, i), 0)) for x in xs]
    operands = list(xs)
    scratch = []
    for w, col0 in ws:
        k = w.shape[1]
        for h in range(sub):
            first = col0 // sw + h
            in_specs.append(pl.BlockSpec(
                (None, k // n_chunks, sw),
                lambda j, i, first=first: (0, i // steps_per_chunk, first + sub * jnp.minimum(j, nj - 1))))
            operands.append(w)
        scratch.append(pltpu.VMEM((2, k, bn), BF16))
    return pl.pallas_call(
        functools.partial(_staged_body, n_x=len(xs), n_w=len(ws), sub=sub, steps_per_chunk=steps_per_chunk,
                          compute=compute),
        grid=(nj + 1, ni),
        in_specs=in_specs + extra_specs,
        out_specs=pl.BlockSpec((bm, bn), lambda j, i: (_tile_row(j, i), _tile_col(j, i))),
        out_shape=jax.ShapeDtypeStruct((m, n), out_dtype),
        scratch_shapes=scratch,
        compiler_params=_params(2),
        name=name,
    )(*operands, *extras)


def _normalize(x, mu, rstd, g, b):
    return (x - mu) * rstd * g + b


def _ln_body(x_ref, g_ref, b_ref, y_ref, *stat_refs):
    x = x_ref[...]
    mu = jnp.mean(x, axis=-1, keepdims=True)
    xc = x - mu
    rstd = lax.rsqrt(jnp.mean(xc * xc, axis=-1, keepdims=True) + LN_EPS)
    y_ref[...] = _normalize(x, mu, rstd, g_ref[...], b_ref[...]).astype(y_ref.dtype)
    if stat_refs:
        mu_ref, rstd_ref = stat_refs
        mu_ref[...] = jnp.broadcast_to(mu, mu_ref.shape)
        rstd_ref[...] = jnp.broadcast_to(rstd, rstd_ref.shape)


def _layer_norm(x, g, b, out_dtype, bm, with_stats):
    m, d = x.shape
    row = pl.BlockSpec((bm, d), lambda i: (i, 0))
    vec = pl.BlockSpec((1, d), lambda i: (0, 0))
    stat = pl.BlockSpec((bm, LANES), lambda i: (i, 0))
    n_stats = 2 if with_stats else 0
    return pl.pallas_call(
        _ln_body,
        grid=(m // bm,),
        in_specs=[row, vec, vec],
        out_specs=[row] + [stat] * n_stats,
        out_shape=[jax.ShapeDtypeStruct((m, d), out_dtype)] + [jax.ShapeDtypeStruct((m, LANES), F32)] * n_stats,
        compiler_params=_params(1),
        name="layer_norm",
    )(x, g.reshape(1, d), b.reshape(1, d))


def _rope_tables(pos):
    inv_freq = ROPE_THETA ** (-jnp.arange(ROPE_HALF, dtype=F32) * 2.0 / ROPE_DIM)
    ang = pos.astype(F32)[:, None] * inv_freq[None, :]
    lane = jnp.arange(LANES) % HEAD_DIM
    cos = jnp.cos(ang)[:, lane % ROPE_HALF]
    sin = jnp.sin(ang)[:, lane % ROPE_HALF]
    c = jnp.where(lane < ROPE_DIM, cos, 1.0)
    s_lo = jnp.where(lane < ROPE_HALF, -sin, 0.0)
    s_hi = jnp.where((lane >= ROPE_HALF) & (lane < ROPE_DIM), sin, 0.0)
    return jnp.stack([c, s_lo, s_hi])


def _rope_chunk(x, tab_ref, rs):
    return (x * tab_ref[0, rs, :]
            + pltpu.roll(x, LANES - ROPE_HALF, 1) * tab_ref[1, rs, :]
            + pltpu.roll(x, ROPE_HALF, 1) * tab_ref[2, rs, :])


def _rope_compute(x_refs, w_refs, extra_refs, o_ref, *, rope_cols, scale):
    (x_ref,), (w_ref,), (tab_ref,) = x_refs, w_refs, extra_refs
    for rs in _row_chunks(x_ref.shape[0]):
        acc = jnp.dot(x_ref[rs, :], w_ref[...], preferred_element_type=F32)
        for c in range(acc.shape[1] // LANES):
            blk = acc[:, c * LANES:(c + 1) * LANES]
            if c * LANES < rope_cols:
                blk = _rope_chunk(blk, tab_ref, rs)
            if scale != 1.0:
                blk = blk * scale
            o_ref[rs, c * LANES:(c + 1) * LANES] = blk.astype(o_ref.dtype)


def _plain_compute(x_refs, w_refs, extra_refs, o_ref):
    (x_ref,), (w_ref,) = x_refs, w_refs
    for rs in _row_chunks(x_ref.shape[0]):
        o_ref[rs, :] = jnp.dot(x_ref[rs, :], w_ref[...], preferred_element_type=F32).astype(o_ref.dtype)


def _gate_compute(x_refs, w_refs, extra_refs, o_ref):
    (x_ref,), (w_ref,), (b_ref,) = x_refs, w_refs, extra_refs
    for rs in _row_chunks(x_ref.shape[0]):
        acc = jnp.dot(x_ref[rs, :], w_ref[...], preferred_element_type=F32)
        o_ref[rs, :] = jax.nn.sigmoid(acc + b_ref[...]).astype(o_ref.dtype)


def _rope_spec(tab, bm):
    t_blocks = tab.shape[1] // bm
    return pl.BlockSpec((3, bm, LANES), lambda j, i: (0, _tile_row(j, i) % t_blocks, 0))


_NT = (((1,), (1,)), ((), ()))


HEADS_PER_DOT = 4
HEADS_PER_VREG = LANES // HEAD_DIM


def _attn_body(sink_ref, q_ref, kvp_ref, kvc_ref, kvm_ref, o_ref):
    i = pl.program_id(1)
    key = lax.broadcasted_iota(jnp.int32, (WINDOW, WINDOW), 0)
    qry = lax.broadcasted_iota(jnp.int32, (WINDOW, WINDOW), 1)
    in_cur = key <= qry
    prev_bias = jnp.where(i > 0, 0.0, NEG_INF).astype(F32)
    lane_slot = lax.broadcasted_iota(jnp.int32, (1, LANES), 1) // HEAD_DIM
    meta_pad = jnp.zeros((LANES - N_META, LANES), F32)

    for kvh in range(N_KV_HEADS):
        blk, slot = divmod(kvh, HEADS_PER_VREG)
        kcols = slice(blk * LANES, (blk + 1) * LANES)
        vcols = slice(KV_WIDTH + blk * LANES, KV_WIDTH + (blk + 1) * LANES)

        k2 = jnp.concatenate([kvp_ref[:, kcols], kvc_ref[:, kcols], kvm_ref[:, kcols]], axis=0).astype(F32)
        k_own = jnp.where(lane_slot == slot, k2, 0.0)
        k_oth = pltpu.roll(k_own, HEAD_DIM, 1)
        k_own, k_oth = k_own.astype(BF16), k_oth.astype(BF16)

        k_par = [jnp.concatenate([k_own if e == slot else k_oth] * HEADS_PER_VREG, axis=1)
                 for e in range(HEADS_PER_VREG)]
        n_band = 2 * WINDOW
        k_band = jnp.concatenate([kp[:n_band] for kp in k_par], axis=0)
        k_meta = jnp.concatenate([kp[n_band:] for kp in k_par], axis=0)

        v2 = jnp.concatenate([kvp_ref[:, vcols], kvc_ref[:, vcols]], axis=0).astype(F32)
        vm2 = jnp.concatenate([kvm_ref[:, vcols].astype(F32), meta_pad], axis=0)
        rows = slice(slot * HEAD_DIM, (slot + 1) * HEAD_DIM)
        v_t = jnp.concatenate([v2.T[rows], vm2.T[rows]], axis=1).astype(BF16)

        for part in range(Q_PER_KV // HEADS_PER_DOT):
            h0 = kvh * Q_PER_KV + part * HEADS_PER_DOT
            qcols = slice(h0 * HEAD_DIM, (h0 + HEADS_PER_DOT) * HEAD_DIM)
            xq = q_ref[:, qcols]
            q_zero = jnp.zeros((WINDOW, LANES), BF16)
            q_bd = jnp.concatenate([jnp.concatenate([xq[:, :LANES], q_zero], axis=1),
                                    jnp.concatenate([q_zero, xq[:, LANES:]], axis=1)], axis=0)
            st = lax.dot_general(k_band, q_bd, _NT, preferred_element_type=F32)
            stm = lax.dot_general(k_meta, q_bd, _NT, preferred_element_type=F32)
            outs = []
            for pair in range(HEADS_PER_DOT // HEADS_PER_VREG):
                qs = slice(pair * WINDOW, (pair + 1) * WINDOW)
                probs, inv = [], []
                for e in range(HEADS_PER_VREG):
                    s_prev = st[e * n_band:e * n_band + WINDOW, qs] + prev_bias
                    s_cur = st[e * n_band + WINDOW:(e + 1) * n_band, qs]
                    s = jnp.where(in_cur, s_cur, s_prev)
                    sm = stm[e * N_META:(e + 1) * N_META, qs]
                    sink = sink_ref[h0 + pair * HEADS_PER_VREG + e]
                    mx = jnp.maximum(jnp.maximum(jnp.max(s, axis=0, keepdims=True),
                                                 jnp.max(sm, axis=0, keepdims=True)), sink)
                    p = jnp.exp(s - mx)
                    pm = jnp.exp(sm - mx)
                    denom = (jnp.sum(p, axis=0, keepdims=True) + jnp.sum(pm, axis=0, keepdims=True)
                             + jnp.exp(sink - mx))
                    probs.append(jnp.concatenate(
                        [jnp.where(in_cur, 0.0, p), jnp.where(in_cur, p, 0.0), pm, meta_pad], axis=0))
                    inv.append(1.0 / denom)
                p_pair = jnp.concatenate(probs, axis=1).astype(BF16)
                o_pair = jnp.dot(v_t, p_pair, preferred_element_type=F32)
                outs += [o_pair[:, e * WINDOW:(e + 1) * WINDOW] * inv[e] for e in range(HEADS_PER_VREG)]
            o_ref[:, qcols] = jnp.concatenate(outs, axis=0).T.astype(o_ref.dtype)


def _attention(q, kv, kv_meta, sinks, batch, seq):
    nb = seq // WINDOW
    return pl.pallas_call(
        _attn_body,
        grid=(batch, nb),
        in_specs=[pl.BlockSpec(memory_space=pltpu.SMEM),
                  pl.BlockSpec((WINDOW, ATTN_WIDTH), lambda b, i: (b * nb + i, 0)),
                  pl.BlockSpec((WINDOW, 2 * KV_WIDTH), lambda b, i: (b * nb + jnp.maximum(i - 1, 0), 0)),
                  pl.BlockSpec((WINDOW, 2 * KV_WIDTH), lambda b, i: (b * nb + i, 0)),
                  pl.BlockSpec((N_META, 2 * KV_WIDTH), lambda b, i: (0, 0))],
        out_specs=pl.BlockSpec((WINDOW, ATTN_WIDTH), lambda b, i: (b * nb + i, 0)),
        out_shape=jax.ShapeDtypeStruct((batch * seq, ATTN_WIDTH), BF16),
        compiler_params=_params(2),
        name="swa_attention",
    )(sinks, q, kv, kv, kv_meta)


POOL_HALO = 16
POOL_PAD = 16
SUBLANES = 8


def _pool_body(up_ref, uc_ref, um_ref, w_ref, sc_ref, o_ref, ext_ref, sa_ref, sb_ref, *, bt):
    i = pl.program_id(1)
    base = POOL_PAD + POOL_HALO
    end = base + bt
    ext_ref[0:POOL_PAD, :] = jnp.zeros((POOL_PAD, POOL_WIDTH), F32)
    ext_ref[POOL_PAD:base, :] = jnp.where(i == 0, um_ref[...], up_ref[...])
    ext_ref[base:, :] = uc_ref[...]
    for g, w in enumerate(POOL_WINDOWS):
        cs = slice(g * POOL_GROUP_WIDTH, (g + 1) * POOL_GROUP_WIDTH)
        src, shift = ext_ref, 1
        for level in range(g):
            start = SUBLANES * (level + 1)
            dst = (sa_ref, sb_ref)[level % 2]
            dst[start:end, cs] = src[start:end, cs] + src[start - shift:end - shift, cs]
            src, shift = dst, 2 * shift
        win = src[base:end, cs] + src[base - shift:end - shift, cs]
        pooled = win / float(w) - uc_ref[:, cs]
        mixed = jnp.dot(pooled.astype(BF16), w_ref[g], preferred_element_type=F32)
        o_ref[:, cs] = (mixed * sc_ref[:, cs]).astype(o_ref.dtype)


def _pool(u, u_meta, w_grp, scale, batch, seq, bt):
    assert all(w == 2 << g for g, w in enumerate(POOL_WINDOWS))
    nb = seq // bt
    halo_per_blk = bt // POOL_HALO
    ext = pltpu.VMEM((POOL_PAD + POOL_HALO + bt, POOL_WIDTH), F32)
    return pl.pallas_call(
        functools.partial(_pool_body, bt=bt),
        grid=(batch, nb),
        in_specs=[pl.BlockSpec((POOL_HALO, POOL_WIDTH),
                               lambda b, i: (jnp.maximum((b * nb + i) * halo_per_blk - 1, 0), 0)),
                  pl.BlockSpec((bt, POOL_WIDTH), lambda b, i: (b * nb + i, 0)),
                  pl.BlockSpec((N_META, POOL_WIDTH), lambda b, i: (0, 0)),
                  pl.BlockSpec((len(POOL_WINDOWS), POOL_GROUP_WIDTH, POOL_GROUP_WIDTH), lambda b, i: (0, 0, 0)),
                  pl.BlockSpec((1, POOL_WIDTH), lambda b, i: (0, 0))],
        out_specs=pl.BlockSpec((bt, POOL_WIDTH), lambda b, i: (b * nb + i, 0)),
        out_shape=jax.ShapeDtypeStruct((batch * seq, POOL_WIDTH), BF16),
        scratch_shapes=[ext, ext, ext],
        compiler_params=_params(2),
        name="multiscale_pool",
    )(u, u, u_meta, w_grp, scale)


def _up_compute(x_refs, w_refs, extra_refs, o_ref):
    (a_ref, p_ref), (wa_ref, wp_ref), (ga_ref, gb_ref) = x_refs, w_refs, extra_refs
    for rs in _row_chunks(a_ref.shape[0]):
        a = jnp.dot(a_ref[rs, :], wa_ref[...], preferred_element_type=F32)
        p = jnp.dot(p_ref[rs, :], wp_ref[...], preferred_element_type=F32)
        o_ref[rs, :] = (ga_ref[rs, :].astype(F32) * a + gb_ref[rs, :].astype(F32) * p).astype(o_ref.dtype)


def _resid_compute(x_refs, w_refs, extra_refs, o_ref):
    (x_ref,), (w_ref,), (r_ref, mu_ref, rstd_ref, g_ref, b_ref) = x_refs, w_refs, extra_refs
    for rs in _row_chunks(x_ref.shape[0]):
        acc = jnp.dot(x_ref[rs, :], w_ref[...], preferred_element_type=F32)
        mu, rstd = mu_ref[rs, :], rstd_ref[rs, :]
        for c in range(acc.shape[1] // LANES):
            cs = slice(c * LANES, (c + 1) * LANES)
            h = _normalize(r_ref[rs, cs], mu, rstd, g_ref[:, cs], b_ref[:, cs])
            o_ref[rs, cs] = DN_ALPHA * h + acc[:, cs]


def _resid_specs(bm, bn, row, col):
    return [pl.BlockSpec((bm, bn), lambda j, i: (row(j, i), col(j, i))),
            pl.BlockSpec((bm, LANES), lambda j, i: (row(j, i), 0)),
            pl.BlockSpec((bm, LANES), lambda j, i: (row(j, i), 0)),
            pl.BlockSpec((1, bn), lambda j, i: (0, col(j, i))),
            pl.BlockSpec((1, bn), lambda j, i: (0, col(j, i)))]


def _swiglu_compute(x_refs, w_refs, extra_refs, o_ref):
    (x_ref,), (wg_ref, wu_ref) = x_refs, w_refs
    for rs in _row_chunks(x_ref.shape[0]):
        x = x_ref[rs, :]
        g = jnp.dot(x, wg_ref[...], preferred_element_type=F32)
        u = jnp.dot(x, wu_ref[...], preferred_element_type=F32)
        o_ref[rs, :] = (jax.nn.silu(g) * u).astype(o_ref.dtype)


def kernel(x, meta_tokens, ln_in_g, ln_in_b, w_in, b_gate, attn_sinks, w_attn_up, w_pool_grp, pool_scale,
           w_pool_up, w_out, ln1_g, ln1_b, w_ffn_in, w_ffn_down, ln2_g, ln2_b):
    batch, seq, d = x.shape
    m = batch * seq
    x2 = x.reshape(m, d)
    q0, k0, _, u0, g0 = IN_SPLITS

    tab_x = _rope_tables(jnp.arange(N_META, N_META + seq))
    tab_m = _rope_tables(jnp.arange(N_META))

    h0b, mu0, rstd0 = _layer_norm(x2, ln_in_g, ln_in_b, BF16, LN_ROW_TILE, True)
    (hmb,) = _layer_norm(meta_tokens, ln_in_g, ln_in_b, BF16, N_META, False)

    rope_q = functools.partial(_rope_compute, rope_cols=ATTN_WIDTH, scale=ATTN_SCALE)
    rope_kv = functools.partial(_rope_compute, rope_cols=KV_WIDTH, scale=1.0)
    wide, narrow = COL_TILE, STAGE_COLS
    q = _staged_matmul(rope_q, [h0b], [(w_in, q0)], [tab_x], [_rope_spec(tab_x, ROW_TILE)], BF16, ATTN_WIDTH,
                       ROW_TILE, wide, "proj_q")
    kv = _staged_matmul(rope_kv, [h0b], [(w_in, k0)], [tab_x], [_rope_spec(tab_x, ROW_TILE)], BF16, 2 * KV_WIDTH,
                        ROW_TILE, narrow, "proj_kv")
    kv_meta = _staged_matmul(rope_kv, [hmb], [(w_in, k0)], [tab_m], [_rope_spec(tab_m, N_META)], BF16,
                             2 * KV_WIDTH, N_META, narrow, "proj_kv_meta")
    u = _staged_matmul(_plain_compute, [h0b], [(w_in, u0)], [], [], F32, POOL_WIDTH, ROW_TILE, wide, "proj_u")
    u_meta = _staged_matmul(_plain_compute, [hmb], [(w_in, u0)], [], [], F32, POOL_WIDTH, N_META, narrow,
                            "proj_u_meta")
    gates = _staged_matmul(_gate_compute, [h0b], [(w_in, g0)], [b_gate[0].reshape(1, 2 * d)],
                           [pl.BlockSpec((1, wide), lambda j, i: (0, _tile_col(j, i)))], BF16, 2 * d,
                           ROW_TILE, wide, "proj_gate")

    attn = _attention(q, kv, kv_meta, attn_sinks[0], batch, seq)
    pool = _pool(u, u_meta, w_pool_grp[0].astype(BF16), pool_scale[0].reshape(1, POOL_WIDTH), batch, seq,
                 POOL_ROW_TILE)

    gate_b0 = d // wide
    mixed = _staged_matmul(
        _up_compute, [attn, pool], [(w_attn_up, 0), (w_pool_up, 0)], [gates, gates],
        [pl.BlockSpec((ROW_TILE, wide), lambda j, i: (_tile_row(j, i), _tile_col(j, i))),
         pl.BlockSpec((ROW_TILE, wide), lambda j, i: (_tile_row(j, i), _tile_col(j, i) + gate_b0))],
        BF16, d, ROW_TILE, wide, "up_merge")
    y1 = _staged_matmul(
        _resid_compute, [mixed], [(w_out, 0)], [x2, mu0, rstd0, ln_in_g.reshape(1, d), ln_in_b.reshape(1, d)],
        _resid_specs(ROW_TILE, wide, _tile_row, _tile_col), F32, d, ROW_TILE, wide, "out_proj")
    h1b, mu1, rstd1 = _layer_norm(y1, ln1_g[0], ln1_b[0], BF16, LN_ROW_TILE, True)

    act = _staged_matmul(_swiglu_compute, [h1b], [(w_ffn_in, 0), (w_ffn_in, D_FF)], [], [], BF16, D_FF,
                         FFN_ROW_TILE, FFN_COL_TILE, "swiglu_in")
    y2 = _staged_matmul(
        _resid_compute, [act], [(w_ffn_down, 0)], [y1, mu1, rstd1, ln1_g[0].reshape(1, d), ln1_b[0].reshape(1, d)],
        _resid_specs(DOWN_ROW_TILE, DOWN_COL_TILE, _tile_row, _tile_col), F32, d, DOWN_ROW_TILE, DOWN_COL_TILE,
        "ffn_down")
    (out,) = _layer_norm(y2, ln2_g[0], ln2_b[0], F32, LN_ROW_TILE, False)
    return out.reshape(batch, seq, d)
```

```python
import functools

import jax
import jax.numpy as jnp
from jax import lax
from jax.experimental import pallas as pl
from jax.experimental.pallas import tpu as pltpu

F32 = jnp.float32
BF16 = jnp.bfloat16

D_MODEL = 4096
N_META = 16
HEAD_DIM = 64
N_Q_HEADS = 32
N_KV_HEADS = 4
Q_PER_KV = N_Q_HEADS // N_KV_HEADS
ATTN_WIDTH = N_Q_HEADS * HEAD_DIM
KV_WIDTH = N_KV_HEADS * HEAD_DIM
WINDOW = 128
ATTN_SCALE = HEAD_DIM ** -0.5
ROPE_DIM = HEAD_DIM // 4
ROPE_HALF = ROPE_DIM // 2
ROPE_THETA = 500000.0
NEG_INF = -1e30
POOL_WINDOWS = (2, 4, 8, 16)
POOL_WIDTH = D_MODEL // 2
POOL_GROUP_WIDTH = POOL_WIDTH // len(POOL_WINDOWS)
IN_SPLITS = (0, ATTN_WIDTH, ATTN_WIDTH + KV_WIDTH, ATTN_WIDTH + 2 * KV_WIDTH,
             ATTN_WIDTH + 2 * KV_WIDTH + POOL_WIDTH)
D_FF = -(-8 * D_MODEL // 768) * 256
DEPTH = 1
DN_ALPHA = (2 * DEPTH) ** 0.25
LN_EPS = 1e-5

LANES = 128
VMEM_LIMIT = 56 * 1024 * 1024

ROW_TILE = 1024
COL_TILE = 1024
STAGE_COLS = 512
MAX_STAGE_CHUNKS = 16
FFN_COL_TILE = 256
FFN_ROW_TILE = 2048
DOWN_ROW_TILE = 512
DOWN_COL_TILE = 512
LN_ROW_TILE = 512
LN_KV_ROW_TILE = 256
POOL_ROW_TILE = 512


def _params(n_grid):
    return pltpu.CompilerParams(dimension_semantics=("arbitrary",) * n_grid, vmem_limit_bytes=VMEM_LIMIT)


MM_ROW_CHUNK = 256


def _row_chunks(bm):
    chunk = min(bm, MM_ROW_CHUNK)
    return [slice(r, r + chunk) for r in range(0, bm, chunk)]


def _tile_row(j, i):
    return jnp.where(j > 0, i, 0)


def _tile_col(j, i):
    return jnp.maximum(j - 1, 0)


def _staged_body(*refs, n_x, n_w, sub, steps_per_chunk, compute):
    x_refs = refs[:n_x]
    w_refs = refs[n_x:n_x + n_w * sub]
    extra_refs = refs[n_x + n_w * sub:len(refs) - n_w - 1]
    o_ref = refs[len(refs) - n_w - 1]
    wb_refs = refs[len(refs) - n_w:]
    j, i = pl.program_id(0), pl.program_id(1)

    def stage():
        for t, wb_ref in enumerate(wb_refs):
            for h in range(sub):
                w_ref = w_refs[t * sub + h]
                kc = w_ref.shape[0]
                rows = pl.ds(pl.multiple_of((i // steps_per_chunk) * kc, kc), kc)
                sw = w_ref.shape[1]
                wb_ref[j % 2, rows, h * sw:(h + 1) * sw] = w_ref[...].astype(BF16)

    @pl.when(j == 0)
    def _():
        stage()

    @pl.when(j > 0)
    def _():
        stage()
        compute(x_refs, [wb_ref.at[(j + 1) % 2] for wb_ref in wb_refs], extra_refs, o_ref)


def _staged_matmul(compute, xs, ws, extras, extra_specs, out_dtype, n, bm, bn, name):
    m = xs[0].shape[0]
    ni, nj = m // bm, n // bn
    sw = min(bn, STAGE_COLS)
    sub = bn // sw
    n_chunks = min(ni, MAX_STAGE_CHUNKS)
    steps_per_chunk = ni // n_chunks
    in_specs = [pl.BlockSpec((bm, x.shape[1]), lambda j, i: (_tile_row(j, i), 0)) for x in xs]
    operands = list(xs)
    scratch = []
    for w, col0 in ws:
        k = w.shape[1]
        for h in range(sub):
            first = col0 // sw + h
            in_specs.append(pl.BlockSpec(
                (None, k // n_chunks, sw),
                lambda j, i, first=first: (0, i // steps_per_chunk, first + sub * jnp.minimum(j, nj - 1))))
            operands.append(w)
        scratch.append(pltpu.VMEM((2, k, bn), BF16))
    return pl.pallas_call(
        functools.partial(_staged_body, n_x=len(xs), n_w=len(ws), sub=sub, steps_per_chunk=steps_per_chunk,
                          compute=compute),
        grid=(nj + 1, ni),
        in_specs=in_specs + extra_specs,
        out_specs=pl.BlockSpec((bm, bn), lambda j, i: (_tile_row(j, i), _tile_col(j, i))),
        out_shape=jax.ShapeDtypeStruct((m, n), out_dtype),
        scratch_shapes=scratch,
        compiler_params=_params(2),
        name=name,
    )(*operands, *extras)


def _normalize(x, mu, rstd, g, b):
    return (x - mu) * rstd * g + b


def _ln_body(x_ref, g_ref, b_ref, y_ref, *stat_refs):
    x = x_ref[...]
    mu = jnp.mean(x, axis=-1, keepdims=True)
    xc = x - mu
    rstd = lax.rsqrt(jnp.mean(xc * xc, axis=-1, keepdims=True) + LN_EPS)
    y_ref[...] = _normalize(x, mu, rstd, g_ref[...], b_ref[...]).astype(y_ref.dtype)
    if stat_refs:
        mu_ref, rstd_ref = stat_refs
        mu_ref[...] = jnp.broadcast_to(mu, mu_ref.shape)
        rstd_ref[...] = jnp.broadcast_to(rstd, rstd_ref.shape)


def _layer_norm(x, g, b, out_dtype, bm, with_stats):
    m, d = x.shape
    row = pl.BlockSpec((bm, d), lambda i: (i, 0))
    vec = pl.BlockSpec((1, d), lambda i: (0, 0))
    stat = pl.BlockSpec((bm, LANES), lambda i: (i, 0))
    n_stats = 2 if with_stats else 0
    return pl.pallas_call(
        _ln_body,
        grid=(m // bm,),
        in_specs=[row, vec, vec],
        out_specs=[row] + [stat] * n_stats,
        out_shape=[jax.ShapeDtypeStruct((m, d), out_dtype)] + [jax.ShapeDtypeStruct((m, LANES), F32)] * n_stats,
        compiler_params=_params(1),
        name="layer_norm",
    )(x, g.reshape(1, d), b.reshape(1, d))


def _rope_tables(pos):
    inv_freq = ROPE_THETA ** (-jnp.arange(ROPE_HALF, dtype=F32) * 2.0 / ROPE_DIM)
    ang = pos.astype(F32)[:, None] * inv_freq[None, :]
    lane = jnp.arange(LANES) % HEAD_DIM
    cos = jnp.cos(ang)[:, lane % ROPE_HALF]
    sin = jnp.sin(ang)[:, lane % ROPE_HALF]
    c = jnp.where(lane < ROPE_DIM, cos, 1.0)
    s_lo = jnp.where(lane < ROPE_HALF, -sin, 0.0)
    s_hi = jnp.where((lane >= ROPE_HALF) & (lane < ROPE_DIM), sin, 0.0)
    return jnp.stack([c, s_lo, s_hi])


def _rope_chunk(x, tab_ref, rs):
    return (x * tab_ref[0, rs, :]
            + pltpu.roll(x, LANES - ROPE_HALF, 1) * tab_ref[1, rs, :]
            + pltpu.roll(x, ROPE_HALF, 1) * tab_ref[2, rs, :])


def _rope_rows(x_ref, w_ref, tab_ref, o_ref, rs, rope_cols, scale):
    acc = jnp.dot(x_ref[rs, :], w_ref[...], preferred_element_type=F32)
    for c in range(acc.shape[1] // LANES):
        blk = acc[:, c * LANES:(c + 1) * LANES]
        if c * LANES < rope_cols:
            blk = _rope_chunk(blk, tab_ref, rs)
        if scale != 1.0:
            blk = blk * scale
        o_ref[rs, c * LANES:(c + 1) * LANES] = blk.astype(o_ref.dtype)


def _rope_compute(x_refs, w_refs, extra_refs, o_ref, *, rope_cols, scale):
    (x_ref,), (w_ref,), (tab_ref,) = x_refs, w_refs, extra_refs
    for rs in _row_chunks(x_ref.shape[0]):
        _rope_rows(x_ref, w_ref, tab_ref, o_ref, rs, rope_cols, scale)


def _plain_compute(x_refs, w_refs, extra_refs, o_ref):
    (x_ref,), (w_ref,) = x_refs, w_refs
    for rs in _row_chunks(x_ref.shape[0]):
        o_ref[rs, :] = jnp.dot(x_ref[rs, :], w_ref[...], preferred_element_type=F32).astype(o_ref.dtype)


def _gate_compute(x_refs, w_refs, extra_refs, o_ref):
    (x_ref,), (w_ref,), (b_ref,) = x_refs, w_refs, extra_refs
    for rs in _row_chunks(x_ref.shape[0]):
        acc = jnp.dot(x_ref[rs, :], w_ref[...], preferred_element_type=F32)
        o_ref[rs, :] = jax.nn.sigmoid(acc + b_ref[...]).astype(o_ref.dtype)


def _rope_spec(tab, bm):
    t_blocks = tab.shape[1] // bm
    return pl.BlockSpec((3, bm, LANES), lambda j, i: (0, _tile_row(j, i) % t_blocks, 0))


def _ln_kv_body(x_ref, g_ref, b_ref, w_ref, tab_ref, h_ref, mu_ref, rstd_ref, kv_ref, even_ref, odd_ref, wb_ref):
    s = pl.program_id(0)

    @pl.when(s == 0)
    def _():
        wb_ref[...] = w_ref[...].astype(BF16)
        odd_ref[...] = jnp.zeros(odd_ref.shape, BF16)

    def step(keep_ref, prev_ref):
        _rope_compute([prev_ref], [wb_ref], [tab_ref], kv_ref, rope_cols=KV_WIDTH, scale=1.0)
        x = x_ref[...]
        mu = jnp.mean(x, axis=-1, keepdims=True)
        xc = x - mu
        rstd = lax.rsqrt(jnp.mean(xc * xc, axis=-1, keepdims=True) + LN_EPS)
        hb = _normalize(x, mu, rstd, g_ref[...], b_ref[...]).astype(BF16)
        h_ref[...] = hb
        keep_ref[...] = hb
        mu_ref[...] = jnp.broadcast_to(mu, mu_ref.shape)
        rstd_ref[...] = jnp.broadcast_to(rstd, rstd_ref.shape)

    @pl.when(s % 2 == 0)
    def _():
        step(even_ref, odd_ref)

    @pl.when(s % 2 == 1)
    def _():
        step(odd_ref, even_ref)


def _ln_kv(x, g, b, w_in, col0, tab, bm):
    m, d = x.shape
    n = 2 * KV_WIDTH
    nt = m // bm
    t_blocks = tab.shape[1] // bm
    ln_tile = lambda s: jnp.minimum(s, nt - 1)
    mm_tile = lambda s: jnp.maximum(s - 1, 0)
    row = pl.BlockSpec((bm, d), lambda s: (ln_tile(s), 0))
    vec = pl.BlockSpec((1, d), lambda s: (0, 0))
    stat = pl.BlockSpec((bm, LANES), lambda s: (ln_tile(s), 0))
    return pl.pallas_call(
        _ln_kv_body,
        grid=(nt + 1,),
        in_specs=[row, vec, vec,
                  pl.BlockSpec((None, d, n), lambda s: (0, 0, col0 // n)),
                  pl.BlockSpec((3, bm, LANES), lambda s: (0, mm_tile(s) % t_blocks, 0))],
        out_specs=[row, stat, stat, pl.BlockSpec((bm, n), lambda s: (mm_tile(s), 0))],
        out_shape=[jax.ShapeDtypeStruct((m, d), BF16), jax.ShapeDtypeStruct((m, LANES), F32),
                   jax.ShapeDtypeStruct((m, LANES), F32), jax.ShapeDtypeStruct((m, n), BF16)],
        scratch_shapes=[pltpu.VMEM((bm, d), BF16), pltpu.VMEM((bm, d), BF16), pltpu.VMEM((d, n), BF16)],
        compiler_params=_params(1),
        name="ln_in_proj_kv",
    )(x, g.reshape(1, d), b.reshape(1, d), w_in, tab)


_NT = (((1,), (1,)), ((), ()))


HEADS_PER_DOT = 4
HEADS_PER_VREG = LANES // HEAD_DIM


def _attn_body(sink_ref, q_ref, kvp_ref, kvc_ref, kvm_ref, o_ref):
    i = pl.program_id(1)
    key = lax.broadcasted_iota(jnp.int32, (WINDOW, WINDOW), 0)
    qry = lax.broadcasted_iota(jnp.int32, (WINDOW, WINDOW), 1)
    in_cur = key <= qry
    prev_bias = jnp.where(i > 0, 0.0, NEG_INF).astype(F32)
    lane_slot = lax.broadcasted_iota(jnp.int32, (1, LANES), 1) // HEAD_DIM
    meta_pad = jnp.zeros((LANES - N_META, LANES), F32)

    n_band = 2 * WINDOW
    quads = []
    for kvh in range(N_KV_HEADS):
        blk, slot = divmod(kvh, HEADS_PER_VREG)
        kcols = slice(blk * LANES, (blk + 1) * LANES)
        vcols = slice(KV_WIDTH + blk * LANES, KV_WIDTH + (blk + 1) * LANES)

        k2 = jnp.concatenate([kvp_ref[:, kcols], kvc_ref[:, kcols], kvm_ref[:, kcols]], axis=0).astype(F32)
        k_own = jnp.where(lane_slot == slot, k2, 0.0)
        k_oth = pltpu.roll(k_own, HEAD_DIM, 1)
        k_own, k_oth = k_own.astype(BF16), k_oth.astype(BF16)

        k_par = [jnp.concatenate([k_own if e == slot else k_oth] * HEADS_PER_VREG, axis=1)
                 for e in range(HEADS_PER_VREG)]
        k_band = jnp.concatenate([kp[:n_band] for kp in k_par], axis=0)
        k_meta = jnp.concatenate([kp[n_band:] for kp in k_par], axis=0)

        v2 = jnp.concatenate([kvp_ref[:, vcols], kvc_ref[:, vcols]], axis=0).astype(F32)
        vm2 = jnp.concatenate([kvm_ref[:, vcols].astype(F32), meta_pad], axis=0)
        rows = slice(slot * HEAD_DIM, (slot + 1) * HEAD_DIM)
        v_t = jnp.concatenate([v2.T[rows], vm2.T[rows]], axis=1).astype(BF16)

        for part in range(Q_PER_KV // HEADS_PER_DOT):
            h0 = kvh * Q_PER_KV + part * HEADS_PER_DOT
            xq = q_ref[:, h0 * HEAD_DIM:(h0 + HEADS_PER_DOT) * HEAD_DIM]
            q_zero = jnp.zeros((WINDOW, LANES), BF16)
            q_bd = jnp.concatenate([jnp.concatenate([xq[:, :LANES], q_zero], axis=1),
                                    jnp.concatenate([q_zero, xq[:, LANES:]], axis=1)], axis=0)
            st = lax.dot_general(k_band, q_bd, _NT, preferred_element_type=F32)
            stm = lax.dot_general(k_meta, q_bd, _NT, preferred_element_type=F32)
            quads.append((h0, st, stm, v_t))

    for h0, st, stm, v_t in quads:
        outs = []
        for pair in range(HEADS_PER_DOT // HEADS_PER_VREG):
            qs = slice(pair * WINDOW, (pair + 1) * WINDOW)
            probs, inv = [], []
            for e in range(HEADS_PER_VREG):
                s_prev = st[e * n_band:e * n_band + WINDOW, qs] + prev_bias
                s_cur = st[e * n_band + WINDOW:(e + 1) * n_band, qs]
                s = jnp.where(in_cur, s_cur, s_prev)
                sm = stm[e * N_META:(e + 1) * N_META, qs]
                sink = sink_ref[h0 + pair * HEADS_PER_VREG + e]
                mx = jnp.maximum(jnp.maximum(jnp.max(s, axis=0, keepdims=True),
                                             jnp.max(sm, axis=0, keepdims=True)), sink)
                p = jnp.exp(s - mx)
                pm = jnp.exp(sm - mx)
                denom = (jnp.sum(p, axis=0, keepdims=True) + jnp.sum(pm, axis=0, keepdims=True)
                         + jnp.exp(sink - mx))
                probs.append(jnp.concatenate(
                    [jnp.where(in_cur, 0.0, p), jnp.where(in_cur, p, 0.0), pm, meta_pad], axis=0))
                inv.append(1.0 / denom)
            p_pair = jnp.concatenate(probs, axis=1).astype(BF16)
            o_pair = jnp.dot(v_t, p_pair, preferred_element_type=F32)
            outs += [o_pair[:, e * WINDOW:(e + 1) * WINDOW] * inv[e] for e in range(HEADS_PER_VREG)]
        qcols = slice(h0 * HEAD_DIM, (h0 + HEADS_PER_DOT) * HEAD_DIM)
        o_ref[:, qcols] = jnp.concatenate(outs, axis=0).T.astype(o_ref.dtype)


def _attention(q, kv, kv_meta, sinks, batch, seq):
    nb = seq // WINDOW
    return pl.pallas_call(
        _attn_body,
        grid=(batch, nb),
        in_specs=[pl.BlockSpec(memory_space=pltpu.SMEM),
                  pl.BlockSpec((WINDOW, ATTN_WIDTH), lambda b, i: (b * nb + i, 0)),
                  pl.BlockSpec((WINDOW, 2 * KV_WIDTH), lambda b, i: (b * nb + jnp.maximum(i - 1, 0), 0)),
                  pl.BlockSpec((WINDOW, 2 * KV_WIDTH), lambda b, i: (b * nb + i, 0)),
                  pl.BlockSpec((N_META, 2 * KV_WIDTH), lambda b, i: (0, 0))],
        out_specs=pl.BlockSpec((WINDOW, ATTN_WIDTH), lambda b, i: (b * nb + i, 0)),
        out_shape=jax.ShapeDtypeStruct((batch * seq, ATTN_WIDTH), BF16),
        compiler_params=_params(2),
        name="swa_attention",
    )(sinks, q, kv, kv, kv_meta)


POOL_HALO = 16
POOL_PAD = 16
SUBLANES = 8


def _pool_body(up_ref, uc_ref, um_ref, w_ref, sc_ref, o_ref, ext_ref, sa_ref, sb_ref, *, bt):
    i = pl.program_id(1)
    base = POOL_PAD + POOL_HALO
    end = base + bt
    ext_ref[0:POOL_PAD, :] = jnp.zeros((POOL_PAD, POOL_WIDTH), F32)
    ext_ref[POOL_PAD:base, :] = jnp.where(i == 0, um_ref[...], up_ref[...])
    ext_ref[base:, :] = uc_ref[...]
    for g, w in enumerate(POOL_WINDOWS):
        cs = slice(g * POOL_GROUP_WIDTH, (g + 1) * POOL_GROUP_WIDTH)
        src, shift = ext_ref, 1
        for level in range(g):
            start = SUBLANES * (level + 1)
            dst = (sa_ref, sb_ref)[level % 2]
            dst[start:end, cs] = src[start:end, cs] + src[start - shift:end - shift, cs]
            src, shift = dst, 2 * shift
        win = src[base:end, cs] + src[base - shift:end - shift, cs]
        pooled = win / float(w) - uc_ref[:, cs]
        mixed = jnp.dot(pooled.astype(BF16), w_ref[g], preferred_element_type=F32)
        o_ref[:, cs] = (mixed * sc_ref[:, cs]).astype(o_ref.dtype)


def _pool(u, u_meta, w_grp, scale, batch, seq, bt):
    assert all(w == 2 << g for g, w in enumerate(POOL_WINDOWS))
    nb = seq // bt
    halo_per_blk = bt // POOL_HALO
    ext = pltpu.VMEM((POOL_PAD + POOL_HALO + bt, POOL_WIDTH), F32)
    return pl.pallas_call(
        functools.partial(_pool_body, bt=bt),
        grid=(batch, nb),
        in_specs=[pl.BlockSpec((POOL_HALO, POOL_WIDTH),
                               lambda b, i: (jnp.maximum((b * nb + i) * halo_per_blk - 1, 0), 0)),
                  pl.BlockSpec((bt, POOL_WIDTH), lambda b, i: (b * nb + i, 0)),
                  pl.BlockSpec((N_META, POOL_WIDTH), lambda b, i: (0, 0)),
                  pl.BlockSpec((len(POOL_WINDOWS), POOL_GROUP_WIDTH, POOL_GROUP_WIDTH), lambda b, i: (0, 0, 0)),
                  pl.BlockSpec((1, POOL_WIDTH), lambda b, i: (0, 0))],
        out_specs=pl.BlockSpec((bt, POOL_WIDTH), lambda b, i: (b * nb + i, 0)),
        out_shape=jax.ShapeDtypeStruct((batch * seq, POOL_WIDTH), BF16),
        scratch_shapes=[ext, ext, ext],
        compiler_params=_params(2),
        name="multiscale_pool",
    )(u, u, u_meta, w_grp, scale)


def _up_compute(x_refs, w_refs, extra_refs, o_ref):
    (a_ref, p_ref), (wa_ref, wp_ref), (ga_ref, gb_ref) = x_refs, w_refs, extra_refs
    for rs in _row_chunks(a_ref.shape[0]):
        a = jnp.dot(a_ref[rs, :], wa_ref[...], preferred_element_type=F32)
        p = jnp.dot(p_ref[rs, :], wp_ref[...], preferred_element_type=F32)
        o_ref[rs, :] = (ga_ref[rs, :].astype(F32) * a + gb_ref[rs, :].astype(F32) * p).astype(o_ref.dtype)


def _resid_compute(x_refs, w_refs, extra_refs, o_ref):
    (x_ref,), (w_ref,), (r_ref, mu_ref, rstd_ref, g_ref, b_ref) = x_refs, w_refs, extra_refs
    for rs in _row_chunks(x_ref.shape[0]):
        acc = jnp.dot(x_ref[rs, :], w_ref[...], preferred_element_type=F32)
        mu, rstd = mu_ref[rs, :], rstd_ref[rs, :]
        for c in range(acc.shape[1] // LANES):
            cs = slice(c * LANES, (c + 1) * LANES)
            h = _normalize(r_ref[rs, cs], mu, rstd, g_ref[:, cs], b_ref[:, cs])
            o_ref[rs, cs] = DN_ALPHA * h + acc[:, cs]


def _resid_specs(bm, bn, row, col):
    return [pl.BlockSpec((bm, bn), lambda j, i: (row(j, i), col(j, i))),
            pl.BlockSpec((bm, LANES), lambda j, i: (row(j, i), 0)),
            pl.BlockSpec((bm, LANES), lambda j, i: (row(j, i), 0)),
            pl.BlockSpec((1, bn), lambda j, i: (0, col(j, i))),
            pl.BlockSpec((1, bn), lambda j, i: (0, col(j, i)))]


def _swiglu_compute(x_refs, w_refs, extra_refs, o_ref):
    (x_ref,), (wg_ref, wu_ref) = x_refs, w_refs
    for rs in _row_chunks(x_ref.shape[0]):
        x = x_ref[rs, :]
        g = jnp.dot(x, wg_ref[...], preferred_element_type=F32)
        u = jnp.dot(x, wu_ref[...], preferred_element_type=F32)
        o_ref[rs, :] = (jax.nn.silu(g) * u).astype(o_ref.dtype)


def kernel(x, meta_tokens, ln_in_g, ln_in_b, w_in, b_gate, attn_sinks, w_attn_up, w_pool_grp, pool_scale,
           w_pool_up, w_out, ln1_g, ln1_b, w_ffn_in, w_ffn_down, ln2_g, ln2_b):
    batch, seq, d = x.shape
    m = batch * seq
    x2 = x.reshape(m, d)
    q0, k0, _, u0, g0 = IN_SPLITS

    tab_x = _rope_tables(jnp.arange(N_META, N_META + seq))
    tab_m = _rope_tables(jnp.arange(N_META))

    h0b, mu0, rstd0, kv = _ln_kv(x2, ln_in_g, ln_in_b, w_in, k0, tab_x, LN_KV_ROW_TILE)
    (hmb,) = _layer_norm(meta_tokens, ln_in_g, ln_in_b, BF16, N_META, False)

    rope_q = functools.partial(_rope_compute, rope_cols=ATTN_WIDTH, scale=ATTN_SCALE)
    rope_kv = functools.partial(_rope_compute, rope_cols=KV_WIDTH, scale=1.0)
    wide, narrow = COL_TILE, STAGE_COLS
    q = _staged_matmul(rope_q, [h0b], [(w_in, q0)], [tab_x], [_rope_spec(tab_x, ROW_TILE)], BF16, ATTN_WIDTH,
                       ROW_TILE, wide, "proj_q")
    kv_meta = _staged_matmul(rope_kv, [hmb], [(w_in, k0)], [tab_m], [_rope_spec(tab_m, N_META)], BF16,
                             2 * KV_WIDTH, N_META, narrow, "proj_kv_meta")
    u = _staged_matmul(_plain_compute, [h0b], [(w_in, u0)], [], [], F32, POOL_WIDTH, ROW_TILE, wide, "proj_u")
    u_meta = _staged_matmul(_plain_compute, [hmb], [(w_in, u0)], [], [], F32, POOL_WIDTH, N_META, narrow,
                            "proj_u_meta")
    gates = _staged_matmul(_gate_compute, [h0b], [(w_in, g0)], [b_gate[0].reshape(1, 2 * d)],
                           [pl.BlockSpec((1, wide), lambda j, i: (0, _tile_col(j, i)))], BF16, 2 * d,
                           ROW_TILE, wide, "proj_gate")

    attn = _attention(q, kv, kv_meta, attn_sinks[0], batch, seq)
    pool = _pool(u, u_meta, w_pool_grp[0].astype(BF16), pool_scale[0].reshape(1, POOL_WIDTH), batch, seq,
                 POOL_ROW_TILE)

    gate_b0 = d // wide
    mixed = _staged_matmul(
        _up_compute, [attn, pool], [(w_attn_up, 0), (w_pool_up, 0)], [gates, gates],
        [pl.BlockSpec((ROW_TILE, wide), lambda j, i: (_tile_row(j, i), _tile_col(j, i))),
         pl.BlockSpec((ROW_TILE, wide), lambda j, i: (_tile_row(j, i), _tile_col(j, i) + gate_b0))],
        BF16, d, ROW_TILE, wide, "up_merge")
    y1 = _staged_matmul(
        _resid_compute, [mixed], [(w_out, 0)], [x2, mu0, rstd0, ln_in_g.reshape(1, d), ln_in_b.reshape(1, d)],
        _resid_specs(ROW_TILE, wide, _tile_row, _tile_col), F32, d, ROW_TILE, wide, "out_proj")
    h1b, mu1, rstd1 = _layer_norm(y1, ln1_g[0], ln1_b[0], BF16, LN_ROW_TILE, True)

    act = _staged_matmul(_swiglu_compute, [h1b], [(w_ffn_in, 0), (w_ffn_in, D_FF)], [], [], BF16, D_FF,
                         FFN_ROW_TILE, FFN_COL_TILE, "swiglu_in")
    y2 = _staged_matmul(
        _resid_compute, [act], [(w_ffn_down, 0)], [y1, mu1, rstd1, ln1_g[0].reshape(1, d), ln1_b[0].reshape(1, d)],
        _resid_specs(DOWN_ROW_TILE, DOWN_COL_TILE, _tile_row, _tile_col), F32, d, DOWN_ROW_TILE, DOWN_COL_TILE,
        "ffn_down")
    (out,) = _layer_norm(y2, ln2_g[0], ln2_b[0], F32, LN_ROW_TILE, False)
    return out.reshape(batch, seq, d)
```

```python
import functools

import jax
import jax.numpy as jnp
from jax import lax
from jax.experimental import pallas as pl
from jax.experimental.pallas import tpu as pltpu

F32 = jnp.float32
BF16 = jnp.bfloat16

D_MODEL = 4096
N_META = 16
HEAD_DIM = 64
N_Q_HEADS = 32
N_KV_HEADS = 4
Q_PER_KV = N_Q_HEADS // N_KV_HEADS
ATTN_WIDTH = N_Q_HEADS * HEAD_DIM
KV_WIDTH = N_KV_HEADS * HEAD_DIM
WINDOW = 128
ATTN_SCALE = HEAD_DIM ** -0.5
ROPE_DIM = HEAD_DIM // 4
ROPE_HALF = ROPE_DIM // 2
ROPE_THETA = 500000.0
NEG_INF = -1e30
POOL_WINDOWS = (2, 4, 8, 16)
POOL_WIDTH = D_MODEL // 2
POOL_GROUP_WIDTH = POOL_WIDTH // len(POOL_WINDOWS)
IN_SPLITS = (0, ATTN_WIDTH, ATTN_WIDTH + KV_WIDTH, ATTN_WIDTH + 2 * KV_WIDTH,
             ATTN_WIDTH + 2 * KV_WIDTH + POOL_WIDTH)
D_FF = -(-8 * D_MODEL // 768) * 256
DEPTH = 1
DN_ALPHA = (2 * DEPTH) ** 0.25
LN_EPS = 1e-5

LANES = 128
VMEM_LIMIT = 56 * 1024 * 1024

ROW_TILE = 1024
COL_TILE = 1024
STAGE_COLS = 512
MAX_STAGE_CHUNKS = 16
FFN_COL_TILE = 256
FFN_ROW_TILE = 2048
DOWN_ROW_TILE = 512
DOWN_COL_TILE = 512
LN_ROW_TILE = 512
LN_KV_ROW_TILE = 256
POOL_ROW_TILE = 512


def _params(n_grid):
    return pltpu.CompilerParams(dimension_semantics=("arbitrary",) * n_grid, vmem_limit_bytes=VMEM_LIMIT)


MM_ROW_CHUNK = 128


def _row_chunks(bm):
    chunk = min(bm, MM_ROW_CHUNK)
    return [slice(r, r + chunk) for r in range(0, bm, chunk)]


def _tile_row(j, i):
    return jnp.where(j > 0, i, 0)


def _tile_col(j, i):
    return jnp.maximum(j - 1, 0)


def _staged_body(*refs, n_x, n_w, sub, steps_per_chunk, compute):
    x_refs = refs[:n_x]
    w_refs = refs[n_x:n_x + n_w * sub]
    extra_refs = refs[n_x + n_w * sub:len(refs) - n_w - 1]
    o_ref = refs[len(refs) - n_w - 1]
    wb_refs = refs[len(refs) - n_w:]
    j, i = pl.program_id(0), pl.program_id(1)

    def stage():
        for t, wb_ref in enumerate(wb_refs):
            for h in range(sub):
                w_ref = w_refs[t * sub + h]
                kc = w_ref.shape[0]
                rows = pl.ds(pl.multiple_of((i // steps_per_chunk) * kc, kc), kc)
                sw = w_ref.shape[1]
                wb_ref[j % 2, rows, h * sw:(h + 1) * sw] = w_ref[...].astype(BF16)

    @pl.when(j == 0)
    def _():
        stage()

    @pl.when(j > 0)
    def _():
        stage()
        compute(x_refs, [wb_ref.at[(j + 1) % 2] for wb_ref in wb_refs], extra_refs, o_ref)


def _staged_matmul(compute, xs, ws, extras, extra_specs, out_dtype, n, bm, bn, name):
    m = xs[0].shape[0]
    ni, nj = m // bm, n // bn
    sw = min(bn, STAGE_COLS)
    sub = bn // sw
    n_chunks = min(ni, MAX_STAGE_CHUNKS)
    steps_per_chunk = ni // n_chunks
    in_specs = [pl.BlockSpec((bm, x.shape[1]), lambda j, i: (_tile_row(j, i), 0)) for x in xs]
    operands = list(xs)
    scratch = []
    for w, col0 in ws:
        k = w.shape[1]
        for h in range(sub):
            first = col0 // sw + h
            in_specs.append(pl.BlockSpec(
                (None, k // n_chunks, sw),
                lambda j, i, first=first: (0, i // steps_per_chunk, first + sub * jnp.minimum(j, nj - 1))))
            operands.append(w)
        scratch.append(pltpu.VMEM((2, k, bn), BF16))
    return pl.pallas_call(
        functools.partial(_staged_body, n_x=len(xs), n_w=len(ws), sub=sub, steps_per_chunk=steps_per_chunk,
                          compute=compute),
        grid=(nj + 1, ni),
        in_specs=in_specs + extra_specs,
        out_specs=pl.BlockSpec((bm, bn), lambda j, i: (_tile_row(j, i), _tile_col(j, i))),
        out_shape=jax.ShapeDtypeStruct((m, n), out_dtype),
        scratch_shapes=scratch,
        compiler_params=_params(2),
        name=name,
    )(*operands, *extras)


def _normalize(x, mu, rstd, g, b):
    return (x - mu) * rstd * g + b


LN_SLAB = 16


def _slab_rows(v):
    return jnp.broadcast_to(v.reshape(1, -1), (LN_SLAB, v.size))


def _ln_body(x_ref, g_ref, b_ref, y_ref, *stat_refs):
    d = x_ref.shape[1]
    slabs = [slice(r, r + LN_SLAB) for r in range(0, x_ref.shape[0], LN_SLAB)]
    cols = [slice(c, c + LANES) for c in range(0, d, LANES)]

    def row_mean(terms):
        acc = functools.reduce(lambda a, t: a + t, terms)
        return jnp.broadcast_to(jnp.sum(acc, axis=-1, keepdims=True) * (1.0 / d), (LN_SLAB, LANES))

    mus = [row_mean([x_ref[rs, cs] for cs in cols]) for rs in slabs]
    rstds = [lax.rsqrt(row_mean([jnp.square(x_ref[rs, cs] - mu) for cs in cols]) + LN_EPS)
             for rs, mu in zip(slabs, mus)]
    for rs, mu, rstd in zip(slabs, mus, rstds):
        for cs in cols:
            y_ref[rs, cs] = _normalize(x_ref[rs, cs], mu, rstd, g_ref[:, cs], b_ref[:, cs]).astype(y_ref.dtype)
        if stat_refs:
            mu_ref, rstd_ref = stat_refs
            mu_ref[rs, :] = mu
            rstd_ref[rs, :] = rstd


def _layer_norm(x, g, b, out_dtype, bm, with_stats):
    m, d = x.shape
    row = pl.BlockSpec((bm, d), lambda i: (i, 0))
    vec = pl.BlockSpec((LN_SLAB, d), lambda i: (0, 0))
    stat = pl.BlockSpec((bm, LANES), lambda i: (i, 0))
    n_stats = 2 if with_stats else 0
    return pl.pallas_call(
        _ln_body,
        grid=(m // bm,),
        in_specs=[row, vec, vec],
        out_specs=[row] + [stat] * n_stats,
        out_shape=[jax.ShapeDtypeStruct((m, d), out_dtype)] + [jax.ShapeDtypeStruct((m, LANES), F32)] * n_stats,
        compiler_params=_params(1),
        name="layer_norm",
    )(x, _slab_rows(g), _slab_rows(b))


def _rope_tables(pos):
    inv_freq = ROPE_THETA ** (-jnp.arange(ROPE_HALF, dtype=F32) * 2.0 / ROPE_DIM)
    ang = pos.astype(F32)[:, None] * inv_freq[None, :]
    lane = jnp.arange(LANES) % HEAD_DIM
    cos = jnp.cos(ang)[:, lane % ROPE_HALF]
    sin = jnp.sin(ang)[:, lane % ROPE_HALF]
    c = jnp.where(lane < ROPE_DIM, cos, 1.0)
    s_lo = jnp.where(lane < ROPE_HALF, -sin, 0.0)
    s_hi = jnp.where((lane >= ROPE_HALF) & (lane < ROPE_DIM), sin, 0.0)
    return jnp.stack([c, s_lo, s_hi])


def _rope_chunk(x, tab_ref, rs):
    return (x * tab_ref[0, rs, :]
            + pltpu.roll(x, LANES - ROPE_HALF, 1) * tab_ref[1, rs, :]
            + pltpu.roll(x, ROPE_HALF, 1) * tab_ref[2, rs, :])


def _rope_rows(x_ref, w_ref, tab_ref, o_ref, rs, rope_cols, scale):
    acc = jnp.dot(x_ref[rs, :], w_ref[...], preferred_element_type=F32)
    for c in range(acc.shape[1] // LANES):
        blk = acc[:, c * LANES:(c + 1) * LANES]
        if c * LANES < rope_cols:
            blk = _rope_chunk(blk, tab_ref, rs)
        if scale != 1.0:
            blk = blk * scale
        o_ref[rs, c * LANES:(c + 1) * LANES] = blk.astype(o_ref.dtype)


def _rope_compute(x_refs, w_refs, extra_refs, o_ref, *, rope_cols, scale):
    (x_ref,), (w_ref,), (tab_ref,) = x_refs, w_refs, extra_refs
    for rs in _row_chunks(x_ref.shape[0]):
        _rope_rows(x_ref, w_ref, tab_ref, o_ref, rs, rope_cols, scale)


def _plain_compute(x_refs, w_refs, extra_refs, o_ref):
    (x_ref,), (w_ref,) = x_refs, w_refs
    for rs in _row_chunks(x_ref.shape[0]):
        o_ref[rs, :] = jnp.dot(x_ref[rs, :], w_ref[...], preferred_element_type=F32).astype(o_ref.dtype)


def _gate_compute(x_refs, w_refs, extra_refs, o_ref):
    (x_ref,), (w_ref,), (b_ref,) = x_refs, w_refs, extra_refs
    for rs in _row_chunks(x_ref.shape[0]):
        acc = jnp.dot(x_ref[rs, :], w_ref[...], preferred_element_type=F32)
        o_ref[rs, :] = jax.nn.sigmoid(acc + b_ref[...]).astype(o_ref.dtype)


def _rope_spec(tab, bm):
    t_blocks = tab.shape[1] // bm
    return pl.BlockSpec((3, bm, LANES), lambda j, i: (0, _tile_row(j, i) % t_blocks, 0))


def _ln_kv_body(x_ref, g_ref, b_ref, w_ref, tab_ref, h_ref, mu_ref, rstd_ref, kv_ref, even_ref, odd_ref, wb_ref):
    s = pl.program_id(0)

    @pl.when(s == 0)
    def _():
        wb_ref[...] = w_ref[...].astype(BF16)
        odd_ref[...] = jnp.zeros(odd_ref.shape, BF16)

    def step(keep_ref, prev_ref):
        _rope_compute([prev_ref], [wb_ref], [tab_ref], kv_ref, rope_cols=KV_WIDTH, scale=1.0)
        x = x_ref[...]
        mu = jnp.mean(x, axis=-1, keepdims=True)
        xc = x - mu
        rstd = lax.rsqrt(jnp.mean(xc * xc, axis=-1, keepdims=True) + LN_EPS)
        hb = _normalize(x, mu, rstd, g_ref[...], b_ref[...]).astype(BF16)
        h_ref[...] = hb
        keep_ref[...] = hb
        mu_ref[...] = jnp.broadcast_to(mu, mu_ref.shape)
        rstd_ref[...] = jnp.broadcast_to(rstd, rstd_ref.shape)

    @pl.when(s % 2 == 0)
    def _():
        step(even_ref, odd_ref)

    @pl.when(s % 2 == 1)
    def _():
        step(odd_ref, even_ref)


def _ln_kv(x, g, b, w_in, col0, tab, bm):
    m, d = x.shape
    n = 2 * KV_WIDTH
    nt = m // bm
    t_blocks = tab.shape[1] // bm
    ln_tile = lambda s: jnp.minimum(s, nt - 1)
    mm_tile = lambda s: jnp.maximum(s - 1, 0)
    row = pl.BlockSpec((bm, d), lambda s: (ln_tile(s), 0))
    vec = pl.BlockSpec((1, d), lambda s: (0, 0))
    stat = pl.BlockSpec((bm, LANES), lambda s: (ln_tile(s), 0))
    return pl.pallas_call(
        _ln_kv_body,
        grid=(nt + 1,),
        in_specs=[row, vec, vec,
                  pl.BlockSpec((None, d, n), lambda s: (0, 0, col0 // n)),
                  pl.BlockSpec((3, bm, LANES), lambda s: (0, mm_tile(s) % t_blocks, 0))],
        out_specs=[row, stat, stat, pl.BlockSpec((bm, n), lambda s: (mm_tile(s), 0))],
        out_shape=[jax.ShapeDtypeStruct((m, d), BF16), jax.ShapeDtypeStruct((m, LANES), F32),
                   jax.ShapeDtypeStruct((m, LANES), F32), jax.ShapeDtypeStruct((m, n), BF16)],
        scratch_shapes=[pltpu.VMEM((bm, d), BF16), pltpu.VMEM((bm, d), BF16), pltpu.VMEM((d, n), BF16)],
        compiler_params=_params(1),
        name="ln_in_proj_kv",
    )(x, g.reshape(1, d), b.reshape(1, d), w_in, tab)


_NT = (((1,), (1,)), ((), ()))


HEADS_PER_DOT = 4
HEADS_PER_VREG = LANES // HEAD_DIM


def _attn_body(sink_ref, q_ref, kvp_ref, kvc_ref, kvm_ref, o_ref):
    i = pl.program_id(1)
    key = lax.broadcasted_iota(jnp.int32, (WINDOW, WINDOW), 0)
    qry = lax.broadcasted_iota(jnp.int32, (WINDOW, WINDOW), 1)
    in_cur = key <= qry
    prev_bias = jnp.where(i > 0, 0.0, NEG_INF).astype(F32)
    lane_slot = lax.broadcasted_iota(jnp.int32, (1, LANES), 1) // HEAD_DIM
    meta_pad = jnp.zeros((LANES - N_META, LANES), F32)

    n_band = 2 * WINDOW
    quads = []
    for kvh in range(N_KV_HEADS):
        blk, slot = divmod(kvh, HEADS_PER_VREG)
        kcols = slice(blk * LANES, (blk + 1) * LANES)
        vcols = slice(KV_WIDTH + blk * LANES, KV_WIDTH + (blk + 1) * LANES)

        k2 = jnp.concatenate([kvp_ref[:, kcols], kvc_ref[:, kcols], kvm_ref[:, kcols]], axis=0).astype(F32)
        k_own = jnp.where(lane_slot == slot, k2, 0.0)
        k_oth = pltpu.roll(k_own, HEAD_DIM, 1)
        k_own, k_oth = k_own.astype(BF16), k_oth.astype(BF16)

        k_par = [jnp.concatenate([k_own if e == slot else k_oth] * HEADS_PER_VREG, axis=1)
                 for e in range(HEADS_PER_VREG)]
        k_band = jnp.concatenate([kp[:n_band] for kp in k_par], axis=0)
        k_meta = jnp.concatenate([kp[n_band:] for kp in k_par], axis=0)

        v2 = jnp.concatenate([kvp_ref[:, vcols], kvc_ref[:, vcols]], axis=0).astype(F32)
        vm2 = jnp.concatenate([kvm_ref[:, vcols].astype(F32), meta_pad], axis=0)
        rows = slice(slot * HEAD_DIM, (slot + 1) * HEAD_DIM)
        v_t = jnp.concatenate([v2.T[rows], vm2.T[rows]], axis=1).astype(BF16)

        for part in range(Q_PER_KV // HEADS_PER_DOT):
            h0 = kvh * Q_PER_KV + part * HEADS_PER_DOT
            xq = q_ref[:, h0 * HEAD_DIM:(h0 + HEADS_PER_DOT) * HEAD_DIM]
            q_zero = jnp.zeros((WINDOW, LANES), BF16)
            q_bd = jnp.concatenate([jnp.concatenate([xq[:, :LANES], q_zero], axis=1),
                                    jnp.concatenate([q_zero, xq[:, LANES:]], axis=1)], axis=0)
            st = lax.dot_general(k_band, q_bd, _NT, preferred_element_type=F32)
            stm = lax.dot_general(k_meta, q_bd, _NT, preferred_element_type=F32)
            quads.append((h0, st, stm, v_t))

    for h0, st, stm, v_t in quads:
        outs = []
        for pair in range(HEADS_PER_DOT // HEADS_PER_VREG):
            qs = slice(pair * WINDOW, (pair + 1) * WINDOW)
            probs, inv = [], []
            for e in range(HEADS_PER_VREG):
                s_prev = st[e * n_band:e * n_band + WINDOW, qs] + prev_bias
                s_cur = st[e * n_band + WINDOW:(e + 1) * n_band, qs]
                s = jnp.where(in_cur, s_cur, s_prev)
                sm = stm[e * N_META:(e + 1) * N_META, qs]
                sink = sink_ref[h0 + pair * HEADS_PER_VREG + e]
                mx = jnp.maximum(jnp.maximum(jnp.max(s, axis=0, keepdims=True),
                                             jnp.max(sm, axis=0, keepdims=True)), sink)
                p = jnp.exp(s - mx)
                pm = jnp.exp(sm - mx)
                denom = (jnp.sum(p, axis=0, keepdims=True) + jnp.sum(pm, axis=0, keepdims=True)
                         + jnp.exp(sink - mx))
                probs.append(jnp.concatenate(
                    [jnp.where(in_cur, 0.0, p), jnp.where(in_cur, p, 0.0), pm, meta_pad], axis=0))
                inv.append(1.0 / denom)
            p_pair = jnp.concatenate(probs, axis=1).astype(BF16)
            o_pair = jnp.dot(v_t, p_pair, preferred_element_type=F32)
            outs += [o_pair[:, e * WINDOW:(e + 1) * WINDOW] * inv[e] for e in range(HEADS_PER_VREG)]
        qcols = slice(h0 * HEAD_DIM, (h0 + HEADS_PER_DOT) * HEAD_DIM)
        o_ref[:, qcols] = jnp.concatenate(outs, axis=0).T.astype(o_ref.dtype)


def _attention(q, kv, kv_meta, sinks, batch, seq):
    nb = seq // WINDOW
    return pl.pallas_call(
        _attn_body,
        grid=(batch, nb),
        in_specs=[pl.BlockSpec(memory_space=pltpu.SMEM),
                  pl.BlockSpec((WINDOW, ATTN_WIDTH), lambda b, i: (b * nb + i, 0)),
                  pl.BlockSpec((WINDOW, 2 * KV_WIDTH), lambda b, i: (b * nb + jnp.maximum(i - 1, 0), 0)),
                  pl.BlockSpec((WINDOW, 2 * KV_WIDTH), lambda b, i: (b * nb + i, 0)),
                  pl.BlockSpec((N_META, 2 * KV_WIDTH), lambda b, i: (0, 0))],
        out_specs=pl.BlockSpec((WINDOW, ATTN_WIDTH), lambda b, i: (b * nb + i, 0)),
        out_shape=jax.ShapeDtypeStruct((batch * seq, ATTN_WIDTH), BF16),
        compiler_params=_params(2),
        name="swa_attention",
    )(sinks, q, kv, kv, kv_meta)


POOL_HALO = 16
POOL_PAD = 16
SUBLANES = 8


def _pool_body(up_ref, uc_ref, um_ref, w_ref, sc_ref, o_ref, ext_ref, sa_ref, sb_ref, *, bt):
    i = pl.program_id(1)
    base = POOL_PAD + POOL_HALO
    end = base + bt
    ext_ref[0:POOL_PAD, :] = jnp.zeros((POOL_PAD, POOL_WIDTH), F32)
    ext_ref[POOL_PAD:base, :] = jnp.where(i == 0, um_ref[...], up_ref[...])
    ext_ref[base:, :] = uc_ref[...]
    for g, w in enumerate(POOL_WINDOWS):
        cs = slice(g * POOL_GROUP_WIDTH, (g + 1) * POOL_GROUP_WIDTH)
        src, shift = ext_ref, 1
        for level in range(g):
            start = SUBLANES * (level + 1)
            dst = (sa_ref, sb_ref)[level % 2]
            dst[start:end, cs] = src[start:end, cs] + src[start - shift:end - shift, cs]
            src, shift = dst, 2 * shift
        win = src[base:end, cs] + src[base - shift:end - shift, cs]
        pooled = win / float(w) - uc_ref[:, cs]
        mixed = jnp.dot(pooled.astype(BF16), w_ref[g], preferred_element_type=F32)
        o_ref[:, cs] = (mixed * sc_ref[:, cs]).astype(o_ref.dtype)


def _pool(u, u_meta, w_grp, scale, batch, seq, bt):
    assert all(w == 2 << g for g, w in enumerate(POOL_WINDOWS))
    nb = seq // bt
    halo_per_blk = bt // POOL_HALO
    ext = pltpu.VMEM((POOL_PAD + POOL_HALO + bt, POOL_WIDTH), F32)
    return pl.pallas_call(
        functools.partial(_pool_body, bt=bt),
        grid=(batch, nb),
        in_specs=[pl.BlockSpec((POOL_HALO, POOL_WIDTH),
                               lambda b, i: (jnp.maximum((b * nb + i) * halo_per_blk - 1, 0), 0)),
                  pl.BlockSpec((bt, POOL_WIDTH), lambda b, i: (b * nb + i, 0)),
                  pl.BlockSpec((N_META, POOL_WIDTH), lambda b, i: (0, 0)),
                  pl.BlockSpec((len(POOL_WINDOWS), POOL_GROUP_WIDTH, POOL_GROUP_WIDTH), lambda b, i: (0, 0, 0)),
                  pl.BlockSpec((1, POOL_WIDTH), lambda b, i: (0, 0))],
        out_specs=pl.BlockSpec((bt, POOL_WIDTH), lambda b, i: (b * nb + i, 0)),
        out_shape=jax.ShapeDtypeStruct((batch * seq, POOL_WIDTH), BF16),
        scratch_shapes=[ext, ext, ext],
        compiler_params=_params(2),
        name="multiscale_pool",
    )(u, u, u_meta, w_grp, scale)


def _up_compute(x_refs, w_refs, extra_refs, o_ref):
    (a_ref, p_ref), (wa_ref, wp_ref), (ga_ref, gb_ref) = x_refs, w_refs, extra_refs
    for rs in _row_chunks(a_ref.shape[0]):
        a = jnp.dot(a_ref[rs, :], wa_ref[...], preferred_element_type=F32)
        p = jnp.dot(p_ref[rs, :], wp_ref[...], preferred_element_type=F32)
        o_ref[rs, :] = (ga_ref[rs, :].astype(F32) * a + gb_ref[rs, :].astype(F32) * p).astype(o_ref.dtype)


def _resid_compute(x_refs, w_refs, extra_refs, o_ref):
    (x_ref,), (w_ref,), (r_ref, mu_ref, rstd_ref, g_ref, b_ref) = x_refs, w_refs, extra_refs
    for rs in _row_chunks(x_ref.shape[0]):
        acc = jnp.dot(x_ref[rs, :], w_ref[...], preferred_element_type=F32)
        mu, rstd = mu_ref[rs, :], rstd_ref[rs, :]
        for c in range(acc.shape[1] // LANES):
            cs = slice(c * LANES, (c + 1) * LANES)
            h = _normalize(r_ref[rs, cs], mu, rstd, g_ref[:, cs], b_ref[:, cs])
            o_ref[rs, cs] = DN_ALPHA * h + acc[:, cs]


def _resid_specs(bm, bn, row, col):
    return [pl.BlockSpec((bm, bn), lambda j, i: (row(j, i), col(j, i))),
            pl.BlockSpec((bm, LANES), lambda j, i: (row(j, i), 0)),
            pl.BlockSpec((bm, LANES), lambda j, i: (row(j, i), 0)),
            pl.BlockSpec((1, bn), lambda j, i: (0, col(j, i))),
            pl.BlockSpec((1, bn), lambda j, i: (0, col(j, i)))]


def _swiglu_compute(x_refs, w_refs, extra_refs, o_ref):
    (x_ref,), (wg_ref, wu_ref) = x_refs, w_refs
    for rs in _row_chunks(x_ref.shape[0]):
        x = x_ref[rs, :]
        g = jnp.dot(x, wg_ref[...], preferred_element_type=F32)
        u = jnp.dot(x, wu_ref[...], preferred_element_type=F32)
        o_ref[rs, :] = (jax.nn.silu(g) * u).astype(o_ref.dtype)


def kernel(x, meta_tokens, ln_in_g, ln_in_b, w_in, b_gate, attn_sinks, w_attn_up, w_pool_grp, pool_scale,
           w_pool_up, w_out, ln1_g, ln1_b, w_ffn_in, w_ffn_down, ln2_g, ln2_b):
    batch, seq, d = x.shape
    m = batch * seq
    x2 = x.reshape(m, d)
    q0, k0, _, u0, g0 = IN_SPLITS

    tab_x = _rope_tables(jnp.arange(N_META, N_META + seq))
    tab_m = _rope_tables(jnp.arange(N_META))

    h0b, mu0, rstd0, kv = _ln_kv(x2, ln_in_g, ln_in_b, w_in, k0, tab_x, LN_KV_ROW_TILE)
    (hmb,) = _layer_norm(meta_tokens, ln_in_g, ln_in_b, BF16, N_META, False)

    rope_q = functools.partial(_rope_compute, rope_cols=ATTN_WIDTH, scale=ATTN_SCALE)
    rope_kv = functools.partial(_rope_compute, rope_cols=KV_WIDTH, scale=1.0)
    wide, narrow = COL_TILE, STAGE_COLS
    q = _staged_matmul(rope_q, [h0b], [(w_in, q0)], [tab_x], [_rope_spec(tab_x, ROW_TILE)], BF16, ATTN_WIDTH,
                       ROW_TILE, wide, "proj_q")
    kv_meta = _staged_matmul(rope_kv, [hmb], [(w_in, k0)], [tab_m], [_rope_spec(tab_m, N_META)], BF16,
                             2 * KV_WIDTH, N_META, narrow, "proj_kv_meta")
    u = _staged_matmul(_plain_compute, [h0b], [(w_in, u0)], [], [], F32, POOL_WIDTH, ROW_TILE, wide, "proj_u")
    u_meta = _staged_matmul(_plain_compute, [hmb], [(w_in, u0)], [], [], F32, POOL_WIDTH, N_META, narrow,
                            "proj_u_meta")
    gates = _staged_matmul(_gate_compute, [h0b], [(w_in, g0)], [b_gate[0].reshape(1, 2 * d)],
                           [pl.BlockSpec((1, wide), lambda j, i: (0, _tile_col(j, i)))], BF16, 2 * d,
                           ROW_TILE, wide, "proj_gate")

    attn = _attention(q, kv, kv_meta, attn_sinks[0], batch, seq)
    pool = _pool(u, u_meta, w_pool_grp[0].astype(BF16), pool_scale[0].reshape(1, POOL_WIDTH), batch, seq,
                 POOL_ROW_TILE)

    gate_b0 = d // wide
    mixed = _staged_matmul(
        _up_compute, [attn, pool], [(w_attn_up, 0), (w_pool_up, 0)], [gates, gates],
        [pl.BlockSpec((ROW_TILE, wide), lambda j, i: (_tile_row(j, i), _tile_col(j, i))),
         pl.BlockSpec((ROW_TILE, wide), lambda j, i: (_tile_row(j, i), _tile_col(j, i) + gate_b0))],
        BF16, d, ROW_TILE, wide, "up_merge")
    y1 = _staged_matmul(
        _resid_compute, [mixed], [(w_out, 0)], [x2, mu0, rstd0, ln_in_g.reshape(1, d), ln_in_b.reshape(1, d)],
        _resid_specs(ROW_TILE, wide, _tile_row, _tile_col), F32, d, ROW_TILE, wide, "out_proj")
    h1b, mu1, rstd1 = _layer_norm(y1, ln1_g[0], ln1_b[0], BF16, LN_ROW_TILE, True)

    act = _staged_matmul(_swiglu_compute, [h1b], [(w_ffn_in, 0), (w_ffn_in, D_FF)], [], [], BF16, D_FF,
                         FFN_ROW_TILE, FFN_COL_TILE, "swiglu_in")
    y2 = _staged_matmul(
        _resid_compute, [act], [(w_ffn_down, 0)], [y1, mu1, rstd1, ln1_g[0].reshape(1, d), ln1_b[0].reshape(1, d)],
        _resid_specs(DOWN_ROW_TILE, DOWN_COL_TILE, _tile_row, _tile_col), F32, d, DOWN_ROW_TILE, DOWN_COL_TILE,
        "ffn_down")
    (out,) = _layer_norm(y2, ln2_g[0], ln2_b[0], F32, LN_ROW_TILE, False)
    return out.reshape(batch, seq, d)
```

```python
import functools

import jax
import jax.numpy as jnp
from jax import lax
from jax.experimental import pallas as pl
from jax.experimental.pallas import tpu as pltpu

F32 = jnp.float32
BF16 = jnp.bfloat16

D_MODEL = 4096
N_META = 16
HEAD_DIM = 64
N_Q_HEADS = 32
N_KV_HEADS = 4
Q_PER_KV = N_Q_HEADS // N_KV_HEADS
ATTN_WIDTH = N_Q_HEADS * HEAD_DIM
KV_WIDTH = N_KV_HEADS * HEAD_DIM
WINDOW = 128
ATTN_SCALE = HEAD_DIM ** -0.5
ROPE_DIM = HEAD_DIM // 4
ROPE_HALF = ROPE_DIM // 2
ROPE_THETA = 500000.0
NEG_INF = -1e30
POOL_WINDOWS = (2, 4, 8, 16)
POOL_WIDTH = D_MODEL // 2
POOL_GROUP_WIDTH = POOL_WIDTH // len(POOL_WINDOWS)
IN_SPLITS = (0, ATTN_WIDTH, ATTN_WIDTH + KV_WIDTH, ATTN_WIDTH + 2 * KV_WIDTH,
             ATTN_WIDTH + 2 * KV_WIDTH + POOL_WIDTH)
D_FF = -(-8 * D_MODEL // 768) * 256
DEPTH = 1
DN_ALPHA = (2 * DEPTH) ** 0.25
LN_EPS = 1e-5

LANES = 128
VMEM_LIMIT = 56 * 1024 * 1024

ROW_TILE = 1024
COL_TILE = 1024
STAGE_COLS = 512
MAX_STAGE_CHUNKS = 16
FFN_COL_TILE = 256
FFN_ROW_TILE = 2048
DOWN_ROW_TILE = 512
DOWN_COL_TILE = 512
LN_ROW_TILE = 512
LN_KV_ROW_TILE = 256
POOL_ROW_TILE = 512


def _params(n_grid):
    return pltpu.CompilerParams(dimension_semantics=("arbitrary",) * n_grid, vmem_limit_bytes=VMEM_LIMIT)


MM_ROW_CHUNK = 256


def _row_chunks(bm):
    chunk = min(bm, MM_ROW_CHUNK)
    return [slice(r, r + chunk) for r in range(0, bm, chunk)]


def _tile_row(j, i):
    return jnp.where(j > 0, i, 0)


def _tile_col(j, i):
    return jnp.maximum(j - 1, 0)


def _staged_body(*refs, n_x, n_w, sub, steps_per_chunk, compute):
    x_refs = refs[:n_x]
    w_refs = refs[n_x:n_x + n_w * sub]
    extra_refs = refs[n_x + n_w * sub:len(refs) - n_w - 1]
    o_ref = refs[len(refs) - n_w - 1]
    wb_refs = refs[len(refs) - n_w:]
    j, i = pl.program_id(0), pl.program_id(1)

    def stage():
        for t, wb_ref in enumerate(wb_refs):
            for h in range(sub):
                w_ref = w_refs[t * sub + h]
                kc = w_ref.shape[0]
                rows = pl.ds(pl.multiple_of((i // steps_per_chunk) * kc, kc), kc)
                sw = w_ref.shape[1]
                wb_ref[j % 2, rows, h * sw:(h + 1) * sw] = w_ref[...].astype(BF16)

    @pl.when(j == 0)
    def _():
        stage()

    @pl.when(j > 0)
    def _():
        stage()
        compute(x_refs, [wb_ref.at[(j + 1) % 2] for wb_ref in wb_refs], extra_refs, o_ref)


def _staged_matmul(compute, xs, ws, extras, extra_specs, out_dtype, n, bm, bn, name):
    m = xs[0].shape[0]
    ni, nj = m // bm, n // bn
    aligned = all(col0 % bn == 0 for _, col0 in ws)
    sw = bn if aligned else min(bn, STAGE_COLS)
    sub = bn // sw
    n_chunks = min(ni, MAX_STAGE_CHUNKS)
    steps_per_chunk = ni // n_chunks
    in_specs = [pl.BlockSpec((bm, x.shape[1]), lambda j, i: (_tile_row(j, i), 0)) for x in xs]
    operands = list(xs)
    scratch = []
    for w, col0 in ws:
        k = w.shape[1]
        for h in range(sub):
            first = col0 // sw + h
            in_specs.append(pl.BlockSpec(
                (None, k // n_chunks, sw),
                lambda j, i, first=first: (0, i // steps_per_chunk, first + sub * jnp.minimum(j, nj - 1))))
            operands.append(w)
        scratch.append(pltpu.VMEM((2, k, bn), BF16))
    return pl.pallas_call(
        functools.partial(_staged_body, n_x=len(xs), n_w=len(ws), sub=sub, steps_per_chunk=steps_per_chunk,
                          compute=compute),
        grid=(nj + 1, ni),
        in_specs=in_specs + extra_specs,
        out_specs=pl.BlockSpec((bm, bn), lambda j, i: (_tile_row(j, i), _tile_col(j, i))),
        out_shape=jax.ShapeDtypeStruct((m, n), out_dtype),
        scratch_shapes=scratch,
        compiler_params=_params(2),
        name=name,
    )(*operands, *extras)


def _normalize(x, mu, rstd, g, b):
    return (x - mu) * rstd * g + b


def _store_row_stats(stat_ref, mu, rstd):
    rows = stat_ref.shape[0]
    stat_ref[:, :LANES] = jnp.broadcast_to(mu, (rows, LANES))
    stat_ref[:, LANES:] = jnp.broadcast_to(rstd, (rows, LANES))


def _ln_body(x_ref, g_ref, b_ref, y_ref, *stat_refs):
    x = x_ref[...]
    mu = jnp.mean(x, axis=-1, keepdims=True)
    xc = x - mu
    rstd = lax.rsqrt(jnp.mean(xc * xc, axis=-1, keepdims=True) + LN_EPS)
    y_ref[...] = _normalize(x, mu, rstd, g_ref[...], b_ref[...]).astype(y_ref.dtype)
    if stat_refs:
        _store_row_stats(stat_refs[0], mu, rstd)


def _layer_norm(x, g, b, out_dtype, bm, with_stats):
    m, d = x.shape
    row = pl.BlockSpec((bm, d), lambda i: (i, 0))
    vec = pl.BlockSpec((1, d), lambda i: (0, 0))
    stat = pl.BlockSpec((bm, 2 * LANES), lambda i: (i, 0))
    n_stats = 1 if with_stats else 0
    return pl.pallas_call(
        _ln_body,
        grid=(m // bm,),
        in_specs=[row, vec, vec],
        out_specs=[row] + [stat] * n_stats,
        out_shape=[jax.ShapeDtypeStruct((m, d), out_dtype)] + [jax.ShapeDtypeStruct((m, 2 * LANES), F32)] * n_stats,
        compiler_params=_params(1),
        name="layer_norm",
    )(x, g.reshape(1, d), b.reshape(1, d))


def _rope_tables(pos):
    inv_freq = ROPE_THETA ** (-jnp.arange(ROPE_HALF, dtype=F32) * 2.0 / ROPE_DIM)
    ang = pos.astype(F32)[:, None] * inv_freq[None, :]
    lane = jnp.arange(LANES) % HEAD_DIM
    cos = jnp.cos(ang)[:, lane % ROPE_HALF]
    sin = jnp.sin(ang)[:, lane % ROPE_HALF]
    c = jnp.where(lane < ROPE_DIM, cos, 1.0)
    s_lo = jnp.where(lane < ROPE_HALF, -sin, 0.0)
    s_hi = jnp.where((lane >= ROPE_HALF) & (lane < ROPE_DIM), sin, 0.0)
    return jnp.stack([c, s_lo, s_hi])


def _rope_chunk(x, tab_ref, rs):
    return (x * tab_ref[0, rs, :]
            + pltpu.roll(x, LANES - ROPE_HALF, 1) * tab_ref[1, rs, :]
            + pltpu.roll(x, ROPE_HALF, 1) * tab_ref[2, rs, :])


def _rope_rows(x_ref, w_ref, tab_ref, o_ref, rs, rope_cols, scale):
    acc = jnp.dot(x_ref[rs, :], w_ref[...], preferred_element_type=F32)
    for c in range(acc.shape[1] // LANES):
        blk = acc[:, c * LANES:(c + 1) * LANES]
        if c * LANES < rope_cols:
            blk = _rope_chunk(blk, tab_ref, rs)
        if scale != 1.0:
            blk = blk * scale
        o_ref[rs, c * LANES:(c + 1) * LANES] = blk.astype(o_ref.dtype)


def _rope_compute(x_refs, w_refs, extra_refs, o_ref, *, rope_cols, scale):
    (x_ref,), (w_ref,), (tab_ref,) = x_refs, w_refs, extra_refs
    for rs in _row_chunks(x_ref.shape[0]):
        _rope_rows(x_ref, w_ref, tab_ref, o_ref, rs, rope_cols, scale)


def _plain_compute(x_refs, w_refs, extra_refs, o_ref):
    (x_ref,), (w_ref,) = x_refs, w_refs
    for rs in _row_chunks(x_ref.shape[0]):
        o_ref[rs, :] = jnp.dot(x_ref[rs, :], w_ref[...], preferred_element_type=F32).astype(o_ref.dtype)


def _gate_compute(x_refs, w_refs, extra_refs, o_ref):
    (x_ref,), (w_ref,), (b_ref,) = x_refs, w_refs, extra_refs
    for rs in _row_chunks(x_ref.shape[0]):
        acc = jnp.dot(x_ref[rs, :], w_ref[...], preferred_element_type=F32)
        o_ref[rs, :] = jax.nn.sigmoid(acc + b_ref[...]).astype(o_ref.dtype)


def _rope_spec(tab, bm):
    t_blocks = tab.shape[1] // bm
    return pl.BlockSpec((3, bm, LANES), lambda j, i: (0, _tile_row(j, i) % t_blocks, 0))


def _ln_kv_body(x_ref, g_ref, b_ref, w_ref, tab_ref, h_ref, stat_ref, kv_ref, even_ref, odd_ref, wb_ref):
    s = pl.program_id(0)

    @pl.when(s == 0)
    def _():
        wb_ref[...] = w_ref[...].astype(BF16)
        odd_ref[...] = jnp.zeros(odd_ref.shape, BF16)

    def step(keep_ref, prev_ref):
        _rope_compute([prev_ref], [wb_ref], [tab_ref], kv_ref, rope_cols=KV_WIDTH, scale=1.0)
        x = x_ref[...]
        mu = jnp.mean(x, axis=-1, keepdims=True)
        xc = x - mu
        rstd = lax.rsqrt(jnp.mean(xc * xc, axis=-1, keepdims=True) + LN_EPS)
        hb = _normalize(x, mu, rstd, g_ref[...], b_ref[...]).astype(BF16)
        h_ref[...] = hb
        keep_ref[...] = hb
        _store_row_stats(stat_ref, mu, rstd)

    @pl.when(s % 2 == 0)
    def _():
        step(even_ref, odd_ref)

    @pl.when(s % 2 == 1)
    def _():
        step(odd_ref, even_ref)


def _ln_kv(x, g, b, w_in, col0, tab, bm):
    m, d = x.shape
    n = 2 * KV_WIDTH
    nt = m // bm
    t_blocks = tab.shape[1] // bm
    ln_tile = lambda s: jnp.minimum(s, nt - 1)
    mm_tile = lambda s: jnp.maximum(s - 1, 0)
    row = pl.BlockSpec((bm, d), lambda s: (ln_tile(s), 0))
    vec = pl.BlockSpec((1, d), lambda s: (0, 0))
    stat = pl.BlockSpec((bm, 2 * LANES), lambda s: (ln_tile(s), 0))
    return pl.pallas_call(
        _ln_kv_body,
        grid=(nt + 1,),
        in_specs=[row, vec, vec,
                  pl.BlockSpec((None, d, n), lambda s: (0, 0, col0 // n)),
                  pl.BlockSpec((3, bm, LANES), lambda s: (0, mm_tile(s) % t_blocks, 0))],
        out_specs=[row, stat, pl.BlockSpec((bm, n), lambda s: (mm_tile(s), 0))],
        out_shape=[jax.ShapeDtypeStruct((m, d), BF16), jax.ShapeDtypeStruct((m, 2 * LANES), F32),
                   jax.ShapeDtypeStruct((m, n), BF16)],
        scratch_shapes=[pltpu.VMEM((bm, d), BF16), pltpu.VMEM((bm, d), BF16), pltpu.VMEM((d, n), BF16)],
        compiler_params=_params(1),
        name="ln_in_proj_kv",
    )(x, g.reshape(1, d), b.reshape(1, d), w_in, tab)


_NT = (((1,), (1,)), ((), ()))


HEADS_PER_DOT = 4
HEADS_PER_VREG = LANES // HEAD_DIM


def _attn_body(sink_ref, q_ref, kvp_ref, kvc_ref, kvm_ref, o_ref):
    i = pl.program_id(1)
    key = lax.broadcasted_iota(jnp.int32, (WINDOW, WINDOW), 0)
    qry = lax.broadcasted_iota(jnp.int32, (WINDOW, WINDOW), 1)
    in_cur = key <= qry
    prev_bias = jnp.where(i > 0, 0.0, NEG_INF).astype(F32)
    lane_slot = lax.broadcasted_iota(jnp.int32, (1, LANES), 1) // HEAD_DIM
    meta_pad = jnp.zeros((LANES - N_META, LANES), F32)

    n_band = 2 * WINDOW
    quads = []
    for kvh in range(N_KV_HEADS):
        blk, slot = divmod(kvh, HEADS_PER_VREG)
        kcols = slice(blk * LANES, (blk + 1) * LANES)
        vcols = slice(KV_WIDTH + blk * LANES, KV_WIDTH + (blk + 1) * LANES)

        k2 = jnp.concatenate([kvp_ref[:, kcols], kvc_ref[:, kcols], kvm_ref[:, kcols]], axis=0).astype(F32)
        k_own = jnp.where(lane_slot == slot, k2, 0.0)
        k_oth = pltpu.roll(k_own, HEAD_DIM, 1)
        k_own, k_oth = k_own.astype(BF16), k_oth.astype(BF16)

        k_par = [jnp.concatenate([k_own if e == slot else k_oth] * HEADS_PER_VREG, axis=1)
                 for e in range(HEADS_PER_VREG)]
        k_band = jnp.concatenate([kp[:n_band] for kp in k_par], axis=0)
        k_meta = jnp.concatenate([kp[n_band:] for kp in k_par], axis=0)

        v2 = jnp.concatenate([kvp_ref[:, vcols], kvc_ref[:, vcols]], axis=0).astype(F32)
        vm2 = jnp.concatenate([kvm_ref[:, vcols].astype(F32), meta_pad], axis=0)
        rows = slice(slot * HEAD_DIM, (slot + 1) * HEAD_DIM)
        v_t = jnp.concatenate([v2.T[rows], vm2.T[rows]], axis=1).astype(BF16)

        for part in range(Q_PER_KV // HEADS_PER_DOT):
            h0 = kvh * Q_PER_KV + part * HEADS_PER_DOT
            xq = q_ref[:, h0 * HEAD_DIM:(h0 + HEADS_PER_DOT) * HEAD_DIM]
            q_zero = jnp.zeros((WINDOW, LANES), BF16)
            q_bd = jnp.concatenate([jnp.concatenate([xq[:, :LANES], q_zero], axis=1),
                                    jnp.concatenate([q_zero, xq[:, LANES:]], axis=1)], axis=0)
            st = lax.dot_general(k_band, q_bd, _NT, preferred_element_type=F32)
            stm = lax.dot_general(k_meta, q_bd, _NT, preferred_element_type=F32)
            quads.append((h0, st, stm, v_t))

    for h0, st, stm, v_t in quads:
        outs = []
        for pair in range(HEADS_PER_DOT // HEADS_PER_VREG):
            qs = slice(pair * WINDOW, (pair + 1) * WINDOW)
            probs, inv = [], []
            for e in range(HEADS_PER_VREG):
                s_prev = st[e * n_band:e * n_band + WINDOW, qs] + prev_bias
                s_cur = st[e * n_band + WINDOW:(e + 1) * n_band, qs]
                s = jnp.where(in_cur, s_cur, s_prev)
                sm = stm[e * N_META:(e + 1) * N_META, qs]
                sink = sink_ref[h0 + pair * HEADS_PER_VREG + e]
                mx = jnp.maximum(jnp.maximum(jnp.max(s, axis=0, keepdims=True),
                                             jnp.max(sm, axis=0, keepdims=True)), sink)
                p = jnp.exp(s - mx)
                pm = jnp.exp(sm - mx)
                denom = (jnp.sum(p, axis=0, keepdims=True) + jnp.sum(pm, axis=0, keepdims=True)
                         + jnp.exp(sink - mx))
                probs.append(jnp.concatenate(
                    [jnp.where(in_cur, 0.0, p), jnp.where(in_cur, p, 0.0), pm, meta_pad], axis=0))
                inv.append(1.0 / denom)
            p_pair = jnp.concatenate(probs, axis=1).astype(BF16)
            o_pair = jnp.dot(v_t, p_pair, preferred_element_type=F32)
            outs += [o_pair[:, e * WINDOW:(e + 1) * WINDOW] * inv[e] for e in range(HEADS_PER_VREG)]
        qcols = slice(h0 * HEAD_DIM, (h0 + HEADS_PER_DOT) * HEAD_DIM)
        o_ref[:, qcols] = jnp.concatenate(outs, axis=0).T.astype(o_ref.dtype)


def _attention(q, kv, kv_meta, sinks, batch, seq):
    nb = seq // WINDOW
    return pl.pallas_call(
        _attn_body,
        grid=(batch, nb),
        in_specs=[pl.BlockSpec(memory_space=pltpu.SMEM),
                  pl.BlockSpec((WINDOW, ATTN_WIDTH), lambda b, i: (b * nb + i, 0)),
                  pl.BlockSpec((WINDOW, 2 * KV_WIDTH), lambda b, i: (b * nb + jnp.maximum(i - 1, 0), 0)),
                  pl.BlockSpec((WINDOW, 2 * KV_WIDTH), lambda b, i: (b * nb + i, 0)),
                  pl.BlockSpec((N_META, 2 * KV_WIDTH), lambda b, i: (0, 0))],
        out_specs=pl.BlockSpec((WINDOW, ATTN_WIDTH), lambda b, i: (b * nb + i, 0)),
        out_shape=jax.ShapeDtypeStruct((batch * seq, ATTN_WIDTH), BF16),
        compiler_params=_params(2),
        name="swa_attention",
    )(sinks, q, kv, kv, kv_meta)


POOL_HALO = 16
POOL_PAD = 16
SUBLANES = 8


def _pool_body(up_ref, uc_ref, um_ref, w_ref, sc_ref, o_ref, ext_ref, sa_ref, sb_ref, *, bt):
    i = pl.program_id(1)
    base = POOL_PAD + POOL_HALO
    end = base + bt
    ext_ref[0:POOL_PAD, :] = jnp.zeros((POOL_PAD, POOL_WIDTH), F32)
    ext_ref[POOL_PAD:base, :] = jnp.where(i == 0, um_ref[...], up_ref[...])
    ext_ref[base:, :] = uc_ref[...]
    for g, w in enumerate(POOL_WINDOWS):
        cs = slice(g * POOL_GROUP_WIDTH, (g + 1) * POOL_GROUP_WIDTH)
        src, shift = ext_ref, 1
        for level in range(g):
            start = SUBLANES * (level + 1)
            dst = (sa_ref, sb_ref)[level % 2]
            dst[start:end, cs] = src[start:end, cs] + src[start - shift:end - shift, cs]
            src, shift = dst, 2 * shift
        win = src[base:end, cs] + src[base - shift:end - shift, cs]
        pooled = win / float(w) - uc_ref[:, cs]
        mixed = jnp.dot(pooled.astype(BF16), w_ref[g], preferred_element_type=F32)
        o_ref[:, cs] = (mixed * sc_ref[:, cs]).astype(o_ref.dtype)


def _pool(u, u_meta, w_grp, scale, batch, seq, bt):
    assert all(w == 2 << g for g, w in enumerate(POOL_WINDOWS))
    nb = seq // bt
    halo_per_blk = bt // POOL_HALO
    ext = pltpu.VMEM((POOL_PAD + POOL_HALO + bt, POOL_WIDTH), F32)
    return pl.pallas_call(
        functools.partial(_pool_body, bt=bt),
        grid=(batch, nb),
        in_specs=[pl.BlockSpec((POOL_HALO, POOL_WIDTH),
                               lambda b, i: (jnp.maximum((b * nb + i) * halo_per_blk - 1, 0), 0)),
                  pl.BlockSpec((bt, POOL_WIDTH), lambda b, i: (b * nb + i, 0)),
                  pl.BlockSpec((N_META, POOL_WIDTH), lambda b, i: (0, 0)),
                  pl.BlockSpec((len(POOL_WINDOWS), POOL_GROUP_WIDTH, POOL_GROUP_WIDTH), lambda b, i: (0, 0, 0)),
                  pl.BlockSpec((1, POOL_WIDTH), lambda b, i: (0, 0))],
        out_specs=pl.BlockSpec((bt, POOL_WIDTH), lambda b, i: (b * nb + i, 0)),
        out_shape=jax.ShapeDtypeStruct((batch * seq, POOL_WIDTH), BF16),
        scratch_shapes=[ext, ext, ext],
        compiler_params=_params(2),
        name="multiscale_pool",
    )(u, u, u_meta, w_grp, scale)


def _up_compute(x_refs, w_refs, extra_refs, o_ref):
    (a_ref, p_ref), (wa_ref, wp_ref), (ga_ref, gb_ref) = x_refs, w_refs, extra_refs
    for rs in _row_chunks(a_ref.shape[0]):
        a = jnp.dot(a_ref[rs, :], wa_ref[...], preferred_element_type=F32)
        p = jnp.dot(p_ref[rs, :], wp_ref[...], preferred_element_type=F32)
        o_ref[rs, :] = (ga_ref[rs, :].astype(F32) * a + gb_ref[rs, :].astype(F32) * p).astype(o_ref.dtype)


def _resid_compute(x_refs, w_refs, extra_refs, o_ref):
    (x_ref,), (w_ref,), (r_ref, stat_ref, gb_ref) = x_refs, w_refs, extra_refs
    for rs in _row_chunks(x_ref.shape[0]):
        acc = jnp.dot(x_ref[rs, :], w_ref[...], preferred_element_type=F32)
        mu, rstd = stat_ref[rs, :LANES], stat_ref[rs, LANES:]
        for c in range(acc.shape[1] // LANES):
            cs = slice(c * LANES, (c + 1) * LANES)
            h = _normalize(r_ref[rs, cs], mu, rstd, gb_ref[0:1, cs], gb_ref[1:2, cs])
            o_ref[rs, cs] = DN_ALPHA * h + acc[:, cs]


def _resid_specs(bm, bn, row, col):
    return [pl.BlockSpec((bm, bn), lambda j, i: (row(j, i), col(j, i))),
            pl.BlockSpec((bm, 2 * LANES), lambda j, i: (row(j, i), 0)),
            pl.BlockSpec((2, bn), lambda j, i: (0, col(j, i)))]


def _swiglu_compute(x_refs, w_refs, extra_refs, o_ref):
    (x_ref,), (wg_ref, wu_ref) = x_refs, w_refs
    for rs in _row_chunks(x_ref.shape[0]):
        x = x_ref[rs, :]
        g = jnp.dot(x, wg_ref[...], preferred_element_type=F32)
        u = jnp.dot(x, wu_ref[...], preferred_element_type=F32)
        o_ref[rs, :] = (jax.nn.silu(g) * u).astype(o_ref.dtype)


def kernel(x, meta_tokens, ln_in_g, ln_in_b, w_in, b_gate, attn_sinks, w_attn_up, w_pool_grp, pool_scale,
           w_pool_up, w_out, ln1_g, ln1_b, w_ffn_in, w_ffn_down, ln2_g, ln2_b):
    batch, seq, d = x.shape
    m = batch * seq
    x2 = x.reshape(m, d)
    q0, k0, _, u0, g0 = IN_SPLITS

    tab_x = _rope_tables(jnp.arange(N_META, N_META + seq))
    tab_m = _rope_tables(jnp.arange(N_META))

    h0b, stat0, kv = _ln_kv(x2, ln_in_g, ln_in_b, w_in, k0, tab_x, LN_KV_ROW_TILE)
    (hmb,) = _layer_norm(meta_tokens, ln_in_g, ln_in_b, BF16, N_META, False)

    rope_q = functools.partial(_rope_compute, rope_cols=ATTN_WIDTH, scale=ATTN_SCALE)
    rope_kv = functools.partial(_rope_compute, rope_cols=KV_WIDTH, scale=1.0)
    wide, narrow = COL_TILE, STAGE_COLS
    q = _staged_matmul(rope_q, [h0b], [(w_in, q0)], [tab_x], [_rope_spec(tab_x, ROW_TILE)], BF16, ATTN_WIDTH,
                       ROW_TILE, wide, "proj_q")
    kv_meta = _staged_matmul(rope_kv, [hmb], [(w_in, k0)], [tab_m], [_rope_spec(tab_m, N_META)], BF16,
                             2 * KV_WIDTH, N_META, narrow, "proj_kv_meta")
    u = _staged_matmul(_plain_compute, [h0b], [(w_in, u0)], [], [], F32, POOL_WIDTH, ROW_TILE, wide, "proj_u")
    u_meta = _staged_matmul(_plain_compute, [hmb], [(w_in, u0)], [], [], F32, POOL_WIDTH, N_META, narrow,
                            "proj_u_meta")
    gates = _staged_matmul(_gate_compute, [h0b], [(w_in, g0)], [b_gate[0].reshape(1, 2 * d)],
                           [pl.BlockSpec((1, wide), lambda j, i: (0, _tile_col(j, i)))], BF16, 2 * d,
                           ROW_TILE, wide, "proj_gate")

    attn = _attention(q, kv, kv_meta, attn_sinks[0], batch, seq)
    pool = _pool(u, u_meta, w_pool_grp[0].astype(BF16), pool_scale[0].reshape(1, POOL_WIDTH), batch, seq,
                 POOL_ROW_TILE)

    gate_b0 = d // wide
    mixed = _staged_matmul(
        _up_compute, [attn, pool], [(w_attn_up, 0), (w_pool_up, 0)], [gates, gates],
        [pl.BlockSpec((ROW_TILE, wide), lambda j, i: (_tile_row(j, i), _tile_col(j, i))),
         pl.BlockSpec((ROW_TILE, wide), lambda j, i: (_tile_row(j, i), _tile_col(j, i) + gate_b0))],
        BF16, d, ROW_TILE, wide, "up_merge")
    y1 = _staged_matmul(
        _resid_compute, [mixed], [(w_out, 0)], [x2, stat0, jnp.stack([ln_in_g, ln_in_b])],
        _resid_specs(ROW_TILE, wide, _tile_row, _tile_col), F32, d, ROW_TILE, wide, "out_proj")
    h1b, stat1 = _layer_norm(y1, ln1_g[0], ln1_b[0], BF16, LN_ROW_TILE, True)

    act = _staged_matmul(_swiglu_compute, [h1b], [(w_ffn_in, 0), (w_ffn_in, D_FF)], [], [], BF16, D_FF,
                         FFN_ROW_TILE, FFN_COL_TILE, "swiglu_in")
    y2 = _staged_matmul(
        _resid_compute, [act], [(w_ffn_down, 0)], [y1, stat1, jnp.stack([ln1_g[0], ln1_b[0]])],
        _resid_specs(DOWN_ROW_TILE, DOWN_COL_TILE, _tile_row, _tile_col), F32, d, DOWN_ROW_TILE, DOWN_COL_TILE,
        "ffn_down")
    (out,) = _layer_norm(y2, ln2_g[0], ln2_b[0], F32, LN_ROW_TILE, False)
    return out.reshape(batch, seq, d)
```

```python
import functools

import jax
import jax.numpy as jnp
from jax import lax
from jax.experimental import pallas as pl
from jax.experimental.pallas import tpu as pltpu

F32 = jnp.float32
BF16 = jnp.bfloat16

D_MODEL = 4096
N_META = 16
HEAD_DIM = 64
N_Q_HEADS = 32
N_KV_HEADS = 4
Q_PER_KV = N_Q_HEADS // N_KV_HEADS
ATTN_WIDTH = N_Q_HEADS * HEAD_DIM
KV_WIDTH = N_KV_HEADS * HEAD_DIM
WINDOW = 128
ATTN_SCALE = HEAD_DIM ** -0.5
ROPE_DIM = HEAD_DIM // 4
ROPE_HALF = ROPE_DIM // 2
ROPE_THETA = 500000.0
NEG_INF = -1e30
POOL_WINDOWS = (2, 4, 8, 16)
POOL_WIDTH = D_MODEL // 2
POOL_GROUP_WIDTH = POOL_WIDTH // len(POOL_WINDOWS)
IN_SPLITS = (0, ATTN_WIDTH, ATTN_WIDTH + KV_WIDTH, ATTN_WIDTH + 2 * KV_WIDTH,
             ATTN_WIDTH + 2 * KV_WIDTH + POOL_WIDTH)
D_FF = -(-8 * D_MODEL // 768) * 256
DEPTH = 1
DN_ALPHA = (2 * DEPTH) ** 0.25
LN_EPS = 1e-5

LANES = 128
VMEM_LIMIT = 56 * 1024 * 1024

ROW_TILE = 1024
COL_TILE = 1024
STAGE_COLS = 512
MAX_STAGE_CHUNKS = 16
FFN_COL_TILE = 256
FFN_ROW_TILE = 2048
DOWN_ROW_TILE = 512
DOWN_COL_TILE = 512
LN_ROW_TILE = 512
LN_KV_ROW_TILE = 256
POOL_ROW_TILE = 512


def _params(n_grid):
    return pltpu.CompilerParams(dimension_semantics=("arbitrary",) * n_grid, vmem_limit_bytes=VMEM_LIMIT)


MM_ROW_CHUNK = 256


def _row_chunks(bm):
    chunk = min(bm, MM_ROW_CHUNK)
    return [slice(r, r + chunk) for r in range(0, bm, chunk)]


def _tile_row(j, i):
    return jnp.where(j > 0, i, 0)


def _tile_col(j, i):
    return jnp.maximum(j - 1, 0)


def _staged_body(*refs, n_x, n_w, sub, steps_per_chunk, compute):
    x_refs = refs[:n_x]
    w_refs = refs[n_x:n_x + n_w * sub]
    extra_refs = refs[n_x + n_w * sub:len(refs) - n_w - 1]
    o_ref = refs[len(refs) - n_w - 1]
    wb_refs = refs[len(refs) - n_w:]
    j, i = pl.program_id(0), pl.program_id(1)

    def stage():
        for t, wb_ref in enumerate(wb_refs):
            for h in range(sub):
                w_ref = w_refs[t * sub + h]
                kc = w_ref.shape[0]
                rows = pl.ds(pl.multiple_of((i // steps_per_chunk) * kc, kc), kc)
                sw = w_ref.shape[1]
                wb_ref[j % 2, rows, h * sw:(h + 1) * sw] = w_ref[...].astype(BF16)

    @pl.when(j == 0)
    def _():
        stage()

    @pl.when(j > 0)
    def _():
        stage()
        compute(x_refs, [wb_ref.at[(j + 1) % 2] for wb_ref in wb_refs], extra_refs, o_ref)


def _staged_matmul(compute, xs, ws, extras, extra_specs, out_dtype, n, bm, bn, name):
    m = xs[0].shape[0]
    ni, nj = m // bm, n // bn
    aligned = all(col0 % bn == 0 for _, col0 in ws)
    sw = bn if aligned else min(bn, STAGE_COLS)
    sub = bn // sw
    n_chunks = min(ni, MAX_STAGE_CHUNKS)
    steps_per_chunk = ni // n_chunks
    in_specs = [pl.BlockSpec((bm, x.shape[1]), lambda j, i: (_tile_row(j, i), 0)) for x in xs]
    operands = list(xs)
    scratch = []
    for w, col0 in ws:
        k = w.shape[1]
        for h in range(sub):
            first = col0 // sw + h
            in_specs.append(pl.BlockSpec(
                (None, k // n_chunks, sw),
                lambda j, i, first=first: (0, i // steps_per_chunk, first + sub * jnp.minimum(j, nj - 1))))
            operands.append(w)
        scratch.append(pltpu.VMEM((2, k, bn), BF16))
    return pl.pallas_call(
        functools.partial(_staged_body, n_x=len(xs), n_w=len(ws), sub=sub, steps_per_chunk=steps_per_chunk,
                          compute=compute),
        grid=(nj + 1, ni),
        in_specs=in_specs + extra_specs,
        out_specs=pl.BlockSpec((bm, bn), lambda j, i: (_tile_row(j, i), _tile_col(j, i))),
        out_shape=jax.ShapeDtypeStruct((m, n), out_dtype),
        scratch_shapes=scratch,
        compiler_params=_params(2),
        name=name,
    )(*operands, *extras)


def _normalize(x, mu, rstd, g, b):
    return (x - mu) * rstd * g + b


def _store_row_stats(stat_ref, mu, rstd):
    rows = stat_ref.shape[0]
    stat_ref[:, :LANES] = jnp.broadcast_to(mu, (rows, LANES))
    stat_ref[:, LANES:] = jnp.broadcast_to(rstd, (rows, LANES))


def _ln_body(x_ref, g_ref, b_ref, y_ref, *stat_refs):
    x = x_ref[...]
    mu = jnp.mean(x, axis=-1, keepdims=True)
    xc = x - mu
    rstd = lax.rsqrt(jnp.mean(xc * xc, axis=-1, keepdims=True) + LN_EPS)
    y_ref[...] = _normalize(x, mu, rstd, g_ref[...], b_ref[...]).astype(y_ref.dtype)
    if stat_refs:
        _store_row_stats(stat_refs[0], mu, rstd)


def _layer_norm(x, g, b, out_dtype, bm, with_stats):
    m, d = x.shape
    row = pl.BlockSpec((bm, d), lambda i: (i, 0))
    vec = pl.BlockSpec((1, d), lambda i: (0, 0))
    stat = pl.BlockSpec((bm, 2 * LANES), lambda i: (i, 0))
    n_stats = 1 if with_stats else 0
    return pl.pallas_call(
        _ln_body,
        grid=(m // bm,),
        in_specs=[row, vec, vec],
        out_specs=[row] + [stat] * n_stats,
        out_shape=[jax.ShapeDtypeStruct((m, d), out_dtype)] + [jax.ShapeDtypeStruct((m, 2 * LANES), F32)] * n_stats,
        compiler_params=_params(1),
        name="layer_norm",
    )(x, g.reshape(1, d), b.reshape(1, d))


def _rope_tables(pos):
    inv_freq = ROPE_THETA ** (-jnp.arange(ROPE_HALF, dtype=F32) * 2.0 / ROPE_DIM)
    ang = pos.astype(F32)[:, None] * inv_freq[None, :]
    lane = jnp.arange(LANES) % HEAD_DIM
    cos = jnp.cos(ang)[:, lane % ROPE_HALF]
    sin = jnp.sin(ang)[:, lane % ROPE_HALF]
    c = jnp.where(lane < ROPE_DIM, cos, 1.0)
    s_lo = jnp.where(lane < ROPE_HALF, -sin, 0.0)
    s_hi = jnp.where((lane >= ROPE_HALF) & (lane < ROPE_DIM), sin, 0.0)
    return jnp.stack([c, s_lo, s_hi])


def _rope_chunk(x, tab_ref, rs):
    return (x * tab_ref[0, rs, :]
            + pltpu.roll(x, LANES - ROPE_HALF, 1) * tab_ref[1, rs, :]
            + pltpu.roll(x, ROPE_HALF, 1) * tab_ref[2, rs, :])


def _rope_rows(x_ref, w_ref, tab_ref, o_ref, rs, rope_cols, scale):
    acc = jnp.dot(x_ref[rs, :], w_ref[...], preferred_element_type=F32)
    for c in range(acc.shape[1] // LANES):
        blk = acc[:, c * LANES:(c + 1) * LANES]
        if c * LANES < rope_cols:
            blk = _rope_chunk(blk, tab_ref, rs)
        if scale != 1.0:
            blk = blk * scale
        o_ref[rs, c * LANES:(c + 1) * LANES] = blk.astype(o_ref.dtype)


def _rope_compute(x_refs, w_refs, extra_refs, o_ref, *, rope_cols, scale):
    (x_ref,), (w_ref,), (tab_ref,) = x_refs, w_refs, extra_refs
    for rs in _row_chunks(x_ref.shape[0]):
        _rope_rows(x_ref, w_ref, tab_ref, o_ref, rs, rope_cols, scale)


def _plain_compute(x_refs, w_refs, extra_refs, o_ref):
    (x_ref,), (w_ref,) = x_refs, w_refs
    for rs in _row_chunks(x_ref.shape[0]):
        o_ref[rs, :] = jnp.dot(x_ref[rs, :], w_ref[...], preferred_element_type=F32).astype(o_ref.dtype)


def _gate_compute(x_refs, w_refs, extra_refs, o_ref):
    (x_ref,), (w_ref,), (b_ref,) = x_refs, w_refs, extra_refs
    for rs in _row_chunks(x_ref.shape[0]):
        acc = jnp.dot(x_ref[rs, :], w_ref[...], preferred_element_type=F32)
        o_ref[rs, :] = jax.nn.sigmoid(acc + b_ref[...]).astype(o_ref.dtype)


def _rope_spec(tab, bm):
    t_blocks = tab.shape[1] // bm
    return pl.BlockSpec((3, bm, LANES), lambda j, i: (0, _tile_row(j, i) % t_blocks, 0))


def _ln_kv_body(x_ref, g_ref, b_ref, w_ref, tab_ref, h_ref, stat_ref, kv_ref, even_ref, odd_ref, wb_ref):
    s = pl.program_id(0)

    @pl.when(s == 0)
    def _():
        wb_ref[...] = w_ref[...].astype(BF16)
        odd_ref[...] = jnp.zeros(odd_ref.shape, BF16)

    def step(keep_ref, prev_ref):
        _rope_compute([prev_ref], [wb_ref], [tab_ref], kv_ref, rope_cols=KV_WIDTH, scale=1.0)
        x = x_ref[...]
        mu = jnp.mean(x, axis=-1, keepdims=True)
        xc = x - mu
        rstd = lax.rsqrt(jnp.mean(xc * xc, axis=-1, keepdims=True) + LN_EPS)
        hb = _normalize(x, mu, rstd, g_ref[...], b_ref[...]).astype(BF16)
        h_ref[...] = hb
        keep_ref[...] = hb
        _store_row_stats(stat_ref, mu, rstd)

    @pl.when(s % 2 == 0)
    def _():
        step(even_ref, odd_ref)

    @pl.when(s % 2 == 1)
    def _():
        step(odd_ref, even_ref)


def _ln_kv(x, g, b, w_in, col0, tab, bm):
    m, d = x.shape
    n = 2 * KV_WIDTH
    nt = m // bm
    t_blocks = tab.shape[1] // bm
    ln_tile = lambda s: jnp.minimum(s, nt - 1)
    mm_tile = lambda s: jnp.maximum(s - 1, 0)
    row = pl.BlockSpec((bm, d), lambda s: (ln_tile(s), 0))
    vec = pl.BlockSpec((1, d), lambda s: (0, 0))
    stat = pl.BlockSpec((bm, 2 * LANES), lambda s: (ln_tile(s), 0))
    return pl.pallas_call(
        _ln_kv_body,
        grid=(nt + 1,),
        in_specs=[row, vec, vec,
                  pl.BlockSpec((None, d, n), lambda s: (0, 0, col0 // n)),
                  pl.BlockSpec((3, bm, LANES), lambda s: (0, mm_tile(s) % t_blocks, 0))],
        out_specs=[row, stat, pl.BlockSpec((bm, n), lambda s: (mm_tile(s), 0))],
        out_shape=[jax.ShapeDtypeStruct((m, d), BF16), jax.ShapeDtypeStruct((m, 2 * LANES), F32),
                   jax.ShapeDtypeStruct((m, n), BF16)],
        scratch_shapes=[pltpu.VMEM((bm, d), BF16), pltpu.VMEM((bm, d), BF16), pltpu.VMEM((d, n), BF16)],
        compiler_params=_params(1),
        name="ln_in_proj_kv",
    )(x, g.reshape(1, d), b.reshape(1, d), w_in, tab)


_NT = (((1,), (1,)), ((), ()))


HEADS_PER_DOT = 4
HEADS_PER_VREG = LANES // HEAD_DIM
ATTN_BLOCKS_PER_STEP = 2


def _attn_body(sink_ref, q_ref, kvp_ref, kvc_ref, kvm_ref, o_ref):
    i = pl.program_id(1)
    key = lax.broadcasted_iota(jnp.int32, (WINDOW, WINDOW), 0)
    qry = lax.broadcasted_iota(jnp.int32, (WINDOW, WINDOW), 1)
    in_cur = key <= qry
    first_prev_bias = jnp.where(i > 0, 0.0, NEG_INF).astype(F32)
    lane_slot = lax.broadcasted_iota(jnp.int32, (1, LANES), 1) // HEAD_DIM
    meta_pad = jnp.zeros((LANES - N_META, LANES), F32)

    n_band = 2 * WINDOW
    quads = []
    for blk in range(q_ref.shape[0] // WINDOW):
        qrows = slice(blk * WINDOW, (blk + 1) * WINDOW)
        prev = (lambda cols: kvp_ref[:, cols]) if blk == 0 else (
            lambda cols, r=slice((blk - 1) * WINDOW, blk * WINDOW): kvc_ref[r, cols])
        prev_bias = first_prev_bias if blk == 0 else None
        for kvh in range(N_KV_HEADS):
            pair_blk, slot = divmod(kvh, HEADS_PER_VREG)
            kcols = slice(pair_blk * LANES, (pair_blk + 1) * LANES)
            vcols = slice(KV_WIDTH + pair_blk * LANES, KV_WIDTH + (pair_blk + 1) * LANES)

            k2 = jnp.concatenate([prev(kcols), kvc_ref[qrows, kcols], kvm_ref[:, kcols]], axis=0).astype(F32)
            k_own = jnp.where(lane_slot == slot, k2, 0.0)
            k_oth = pltpu.roll(k_own, HEAD_DIM, 1)
            k_own, k_oth = k_own.astype(BF16), k_oth.astype(BF16)

            k_par = [jnp.concatenate([k_own if e == slot else k_oth] * HEADS_PER_VREG, axis=1)
                     for e in range(HEADS_PER_VREG)]
            k_band = jnp.concatenate([kp[:n_band] for kp in k_par], axis=0)
            k_meta = jnp.concatenate([kp[n_band:] for kp in k_par], axis=0)

            v2 = jnp.concatenate([prev(vcols), kvc_ref[qrows, vcols]], axis=0).astype(F32)
            vm2 = jnp.concatenate([kvm_ref[:, vcols].astype(F32), meta_pad], axis=0)
            rows = slice(slot * HEAD_DIM, (slot + 1) * HEAD_DIM)
            v_t = jnp.concatenate([v2.T[rows], vm2.T[rows]], axis=1).astype(BF16)

            for part in range(Q_PER_KV // HEADS_PER_DOT):
                h0 = kvh * Q_PER_KV + part * HEADS_PER_DOT
                xq = q_ref[qrows, h0 * HEAD_DIM:(h0 + HEADS_PER_DOT) * HEAD_DIM]
                q_zero = jnp.zeros((WINDOW, LANES), BF16)
                q_bd = jnp.concatenate([jnp.concatenate([xq[:, :LANES], q_zero], axis=1),
                                        jnp.concatenate([q_zero, xq[:, LANES:]], axis=1)], axis=0)
                st = lax.dot_general(k_band, q_bd, _NT, preferred_element_type=F32)
                stm = lax.dot_general(k_meta, q_bd, _NT, preferred_element_type=F32)
                quads.append((qrows, prev_bias, h0, st, stm, v_t))

    for qrows, prev_bias, h0, st, stm, v_t in quads:
        outs = []
        for pair in range(HEADS_PER_DOT // HEADS_PER_VREG):
            qs = slice(pair * WINDOW, (pair + 1) * WINDOW)
            probs, inv = [], []
            for e in range(HEADS_PER_VREG):
                s_prev = st[e * n_band:e * n_band + WINDOW, qs]
                if prev_bias is not None:
                    s_prev = s_prev + prev_bias
                s_cur = st[e * n_band + WINDOW:(e + 1) * n_band, qs]
                s = jnp.where(in_cur, s_cur, s_prev)
                sm = stm[e * N_META:(e + 1) * N_META, qs]
                sink = sink_ref[h0 + pair * HEADS_PER_VREG + e]
                mx = jnp.maximum(jnp.maximum(jnp.max(s, axis=0, keepdims=True),
                                             jnp.max(sm, axis=0, keepdims=True)), sink)
                p = jnp.exp(s - mx)
                pm = jnp.exp(sm - mx)
                denom = (jnp.sum(p, axis=0, keepdims=True) + jnp.sum(pm, axis=0, keepdims=True)
                         + jnp.exp(sink - mx))
                probs.append(jnp.concatenate(
                    [jnp.where(in_cur, 0.0, p), jnp.where(in_cur, p, 0.0), pm, meta_pad], axis=0))
                inv.append(1.0 / denom)
            p_pair = jnp.concatenate(probs, axis=1).astype(BF16)
            o_pair = jnp.dot(v_t, p_pair, preferred_element_type=F32)
            outs += [o_pair[:, e * WINDOW:(e + 1) * WINDOW] * inv[e] for e in range(HEADS_PER_VREG)]
        qcols = slice(h0 * HEAD_DIM, (h0 + HEADS_PER_DOT) * HEAD_DIM)
        o_ref[qrows, qcols] = jnp.concatenate(outs, axis=0).T.astype(o_ref.dtype)


def _attention(q, kv, kv_meta, sinks, batch, seq):
    bq = ATTN_BLOCKS_PER_STEP * WINDOW
    nb = seq // bq
    halo_per_blk = ATTN_BLOCKS_PER_STEP
    return pl.pallas_call(
        _attn_body,
        grid=(batch, nb),
        in_specs=[pl.BlockSpec(memory_space=pltpu.SMEM),
                  pl.BlockSpec((bq, ATTN_WIDTH), lambda b, i: (b * nb + i, 0)),
                  pl.BlockSpec((WINDOW, 2 * KV_WIDTH),
                               lambda b, i: (jnp.maximum((b * nb + i) * halo_per_blk - 1, 0), 0)),
                  pl.BlockSpec((bq, 2 * KV_WIDTH), lambda b, i: (b * nb + i, 0)),
                  pl.BlockSpec((N_META, 2 * KV_WIDTH), lambda b, i: (0, 0))],
        out_specs=pl.BlockSpec((bq, ATTN_WIDTH), lambda b, i: (b * nb + i, 0)),
        out_shape=jax.ShapeDtypeStruct((batch * seq, ATTN_WIDTH), BF16),
        compiler_params=_params(2),
        name="swa_attention",
    )(sinks, q, kv, kv, kv_meta)


POOL_HALO = 16
POOL_PAD = 16
SUBLANES = 8


def _pool_body(up_ref, uc_ref, um_ref, w_ref, sc_ref, o_ref, ext_ref, sa_ref, sb_ref, *, bt):
    i = pl.program_id(1)
    base = POOL_PAD + POOL_HALO
    end = base + bt
    ext_ref[0:POOL_PAD, :] = jnp.zeros((POOL_PAD, POOL_WIDTH), F32)
    ext_ref[POOL_PAD:base, :] = jnp.where(i == 0, um_ref[...], up_ref[...])
    ext_ref[base:, :] = uc_ref[...]
    for g, w in enumerate(POOL_WINDOWS):
        cs = slice(g * POOL_GROUP_WIDTH, (g + 1) * POOL_GROUP_WIDTH)
        src, shift = ext_ref, 1
        for level in range(g):
            start = SUBLANES * (level + 1)
            dst = (sa_ref, sb_ref)[level % 2]
            dst[start:end, cs] = src[start:end, cs] + src[start - shift:end - shift, cs]
            src, shift = dst, 2 * shift
        win = src[base:end, cs] + src[base - shift:end - shift, cs]
        pooled = win / float(w) - uc_ref[:, cs]
        mixed = jnp.dot(pooled.astype(BF16), w_ref[g], preferred_element_type=F32)
        o_ref[:, cs] = (mixed * sc_ref[:, cs]).astype(o_ref.dtype)


def _pool(u, u_meta, w_grp, scale, batch, seq, bt):
    assert all(w == 2 << g for g, w in enumerate(POOL_WINDOWS))
    nb = seq // bt
    halo_per_blk = bt // POOL_HALO
    ext = pltpu.VMEM((POOL_PAD + POOL_HALO + bt, POOL_WIDTH), F32)
    return pl.pallas_call(
        functools.partial(_pool_body, bt=bt),
        grid=(batch, nb),
        in_specs=[pl.BlockSpec((POOL_HALO, POOL_WIDTH),
                               lambda b, i: (jnp.maximum((b * nb + i) * halo_per_blk - 1, 0), 0)),
                  pl.BlockSpec((bt, POOL_WIDTH), lambda b, i: (b * nb + i, 0)),
                  pl.BlockSpec((N_META, POOL_WIDTH), lambda b, i: (0, 0)),
                  pl.BlockSpec((len(POOL_WINDOWS), POOL_GROUP_WIDTH, POOL_GROUP_WIDTH), lambda b, i: (0, 0, 0)),
                  pl.BlockSpec((1, POOL_WIDTH), lambda b, i: (0, 0))],
        out_specs=pl.BlockSpec((bt, POOL_WIDTH), lambda b, i: (b * nb + i, 0)),
        out_shape=jax.ShapeDtypeStruct((batch * seq, POOL_WIDTH), BF16),
        scratch_shapes=[ext, ext, ext],
        compiler_params=_params(2),
        name="multiscale_pool",
    )(u, u, u_meta, w_grp, scale)


def _up_compute(x_refs, w_refs, extra_refs, o_ref):
    (a_ref, p_ref), (wa_ref, wp_ref), (ga_ref, gb_ref) = x_refs, w_refs, extra_refs
    for rs in _row_chunks(a_ref.shape[0]):
        a = jnp.dot(a_ref[rs, :], wa_ref[...], preferred_element_type=F32)
        p = jnp.dot(p_ref[rs, :], wp_ref[...], preferred_element_type=F32)
        o_ref[rs, :] = (ga_ref[rs, :].astype(F32) * a + gb_ref[rs, :].astype(F32) * p).astype(o_ref.dtype)


def _resid_compute(x_refs, w_refs, extra_refs, o_ref):
    (x_ref,), (w_ref,), (r_ref, stat_ref, gb_ref) = x_refs, w_refs, extra_refs
    for rs in _row_chunks(x_ref.shape[0]):
        acc = jnp.dot(x_ref[rs, :], w_ref[...], preferred_element_type=F32)
        mu, rstd = stat_ref[rs, :LANES], stat_ref[rs, LANES:]
        for c in range(acc.shape[1] // LANES):
            cs = slice(c * LANES, (c + 1) * LANES)
            h = _normalize(r_ref[rs, cs], mu, rstd, gb_ref[0:1, cs], gb_ref[1:2, cs])
            o_ref[rs, cs] = DN_ALPHA * h + acc[:, cs]


def _resid_specs(bm, bn, row, col):
    return [pl.BlockSpec((bm, bn), lambda j, i: (row(j, i), col(j, i))),
            pl.BlockSpec((bm, 2 * LANES), lambda j, i: (row(j, i), 0)),
            pl.BlockSpec((2, bn), lambda j, i: (0, col(j, i)))]


def _swiglu_compute(x_refs, w_refs, extra_refs, o_ref):
    (x_ref,), (wg_ref, wu_ref) = x_refs, w_refs
    for rs in _row_chunks(x_ref.shape[0]):
        x = x_ref[rs, :]
        g = jnp.dot(x, wg_ref[...], preferred_element_type=F32)
        u = jnp.dot(x, wu_ref[...], preferred_element_type=F32)
        o_ref[rs, :] = (jax.nn.silu(g) * u).astype(o_ref.dtype)


def kernel(x, meta_tokens, ln_in_g, ln_in_b, w_in, b_gate, attn_sinks, w_attn_up, w_pool_grp, pool_scale,
           w_pool_up, w_out, ln1_g, ln1_b, w_ffn_in, w_ffn_down, ln2_g, ln2_b):
    batch, seq, d = x.shape
    m = batch * seq
    x2 = x.reshape(m, d)
    q0, k0, _, u0, g0 = IN_SPLITS

    tab_x = _rope_tables(jnp.arange(N_META, N_META + seq))
    tab_m = _rope_tables(jnp.arange(N_META))

    h0b, stat0, kv = _ln_kv(x2, ln_in_g, ln_in_b, w_in, k0, tab_x, LN_KV_ROW_TILE)
    (hmb,) = _layer_norm(meta_tokens, ln_in_g, ln_in_b, BF16, N_META, False)

    rope_q = functools.partial(_rope_compute, rope_cols=ATTN_WIDTH, scale=ATTN_SCALE)
    rope_kv = functools.partial(_rope_compute, rope_cols=KV_WIDTH, scale=1.0)
    wide, narrow = COL_TILE, STAGE_COLS
    q = _staged_matmul(rope_q, [h0b], [(w_in, q0)], [tab_x], [_rope_spec(tab_x, ROW_TILE)], BF16, ATTN_WIDTH,
                       ROW_TILE, wide, "proj_q")
    kv_meta = _staged_matmul(rope_kv, [hmb], [(w_in, k0)], [tab_m], [_rope_spec(tab_m, N_META)], BF16,
                             2 * KV_WIDTH, N_META, narrow, "proj_kv_meta")
    u = _staged_matmul(_plain_compute, [h0b], [(w_in, u0)], [], [], F32, POOL_WIDTH, ROW_TILE, wide, "proj_u")
    u_meta = _staged_matmul(_plain_compute, [hmb], [(w_in, u0)], [], [], F32, POOL_WIDTH, N_META, narrow,
                            "proj_u_meta")
    gates = _staged_matmul(_gate_compute, [h0b], [(w_in, g0)], [b_gate[0].reshape(1, 2 * d)],
                           [pl.BlockSpec((1, wide), lambda j, i: (0, _tile_col(j, i)))], BF16, 2 * d,
                           ROW_TILE, wide, "proj_gate")

    attn = _attention(q, kv, kv_meta, attn_sinks[0], batch, seq)
    pool = _pool(u, u_meta, w_pool_grp[0].astype(BF16), pool_scale[0].reshape(1, POOL_WIDTH), batch, seq,
                 POOL_ROW_TILE)

    gate_b0 = d // wide
    mixed = _staged_matmul(
        _up_compute, [attn, pool], [(w_attn_up, 0), (w_pool_up, 0)], [gates, gates],
        [pl.BlockSpec((ROW_TILE, wide), lambda j, i: (_tile_row(j, i), _tile_col(j, i))),
         pl.BlockSpec((ROW_TILE, wide), lambda j, i: (_tile_row(j, i), _tile_col(j, i) + gate_b0))],
        BF16, d, ROW_TILE, wide, "up_merge")
    y1 = _staged_matmul(
        _resid_compute, [mixed], [(w_out, 0)], [x2, stat0, jnp.stack([ln_in_g, ln_in_b])],
        _resid_specs(ROW_TILE, wide, _tile_row, _tile_col), F32, d, ROW_TILE, wide, "out_proj")
    h1b, stat1 = _layer_norm(y1, ln1_g[0], ln1_b[0], BF16, LN_ROW_TILE, True)

    act = _staged_matmul(_swiglu_compute, [h1b], [(w_ffn_in, 0), (w_ffn_in, D_FF)], [], [], BF16, D_FF,
                         FFN_ROW_TILE, FFN_COL_TILE, "swiglu_in")
    y2 = _staged_matmul(
        _resid_compute, [act], [(w_ffn_down, 0)], [y1, stat1, jnp.stack([ln1_g[0], ln1_b[0]])],
        _resid_specs(DOWN_ROW_TILE, DOWN_COL_TILE, _tile_row, _tile_col), F32, d, DOWN_ROW_TILE, DOWN_COL_TILE,
        "ffn_down")
    (out,) = _layer_norm(y2, ln2_g[0], ln2_b[0], F32, LN_ROW_TILE, False)
    return out.reshape(batch, seq, d)
```

```python
import functools

import jax
import jax.numpy as jnp
from jax import lax
from jax.experimental import pallas as pl
from jax.experimental.pallas import tpu as pltpu

F32 = jnp.float32
BF16 = jnp.bfloat16

D_MODEL = 4096
N_META = 16
HEAD_DIM = 64
N_Q_HEADS = 32
N_KV_HEADS = 4
Q_PER_KV = N_Q_HEADS // N_KV_HEADS
ATTN_WIDTH = N_Q_HEADS * HEAD_DIM
KV_WIDTH = N_KV_HEADS * HEAD_DIM
WINDOW = 128
ATTN_SCALE = HEAD_DIM ** -0.5
ROPE_DIM = HEAD_DIM // 4
ROPE_HALF = ROPE_DIM // 2
ROPE_THETA = 500000.0
NEG_INF = -1e30
POOL_WINDOWS = (2, 4, 8, 16)
POOL_WIDTH = D_MODEL // 2
POOL_GROUP_WIDTH = POOL_WIDTH // len(POOL_WINDOWS)
IN_SPLITS = (0, ATTN_WIDTH, ATTN_WIDTH + KV_WIDTH, ATTN_WIDTH + 2 * KV_WIDTH,
             ATTN_WIDTH + 2 * KV_WIDTH + POOL_WIDTH)
D_FF = -(-8 * D_MODEL // 768) * 256
DEPTH = 1
DN_ALPHA = (2 * DEPTH) ** 0.25
LN_EPS = 1e-5

LANES = 128
VMEM_LIMIT = 56 * 1024 * 1024

ROW_TILE = 1024
COL_TILE = 1024
STAGE_COLS = 512
MAX_STAGE_CHUNKS = 16
FFN_COL_TILE = 256
FFN_ROW_TILE = 2048
DOWN_ROW_TILE = 512
DOWN_COL_TILE = 512
LN_ROW_TILE = 512
LN_KV_ROW_TILE = 256
POOL_ROW_TILE = 512


def _params(n_grid):
    return pltpu.CompilerParams(dimension_semantics=("arbitrary",) * n_grid, vmem_limit_bytes=VMEM_LIMIT)


MM_ROW_CHUNK = 256


def _row_chunks(bm):
    chunk = min(bm, MM_ROW_CHUNK)
    return [slice(r, r + chunk) for r in range(0, bm, chunk)]


def _tile_row(j, i):
    return jnp.where(j > 0, i, 0)


def _tile_col(j, i):
    return jnp.maximum(j - 1, 0)


def _staged_body(*refs, n_x, n_w, sub, steps_per_chunk, compute):
    x_refs = refs[:n_x]
    w_refs = refs[n_x:n_x + n_w * sub]
    extra_refs = refs[n_x + n_w * sub:len(refs) - n_w - 1]
    o_ref = refs[len(refs) - n_w - 1]
    wb_refs = refs[len(refs) - n_w:]
    j, i = pl.program_id(0), pl.program_id(1)

    def stage():
        for t, wb_ref in enumerate(wb_refs):
            for h in range(sub):
                w_ref = w_refs[t * sub + h]
                kc = w_ref.shape[0]
                rows = pl.ds(pl.multiple_of((i // steps_per_chunk) * kc, kc), kc)
                sw = w_ref.shape[1]
                wb_ref[j % 2, rows, h * sw:(h + 1) * sw] = w_ref[...].astype(BF16)

    @pl.when(j == 0)
    def _():
        stage()

    @pl.when(j > 0)
    def _():
        stage()
        compute(x_refs, [wb_ref.at[(j + 1) % 2] for wb_ref in wb_refs], extra_refs, o_ref)


def _staged_matmul(compute, xs, ws, extras, extra_specs, out_dtype, n, bm, bn, name):
    m = xs[0].shape[0]
    ni, nj = m // bm, n // bn
    aligned = all(col0 % bn == 0 for _, col0 in ws)
    sw = bn if aligned else min(bn, STAGE_COLS)
    sub = bn // sw
    n_chunks = min(ni, MAX_STAGE_CHUNKS)
    steps_per_chunk = ni // n_chunks
    in_specs = [pl.BlockSpec((bm, x.shape[1]), lambda j, i: (_tile_row(j, i), 0)) for x in xs]
    operands = list(xs)
    scratch = []
    for w, col0 in ws:
        k = w.shape[1]
        for h in range(sub):
            first = col0 // sw + h
            in_specs.append(pl.BlockSpec(
                (None, k // n_chunks, sw),
                lambda j, i, first=first: (0, i // steps_per_chunk, first + sub * jnp.minimum(j, nj - 1))))
            operands.append(w)
        scratch.append(pltpu.VMEM((2, k, bn), BF16))
    return pl.pallas_call(
        functools.partial(_staged_body, n_x=len(xs), n_w=len(ws), sub=sub, steps_per_chunk=steps_per_chunk,
                          compute=compute),
        grid=(nj + 1, ni),
        in_specs=in_specs + extra_specs,
        out_specs=pl.BlockSpec((bm, bn), lambda j, i: (_tile_row(j, i), _tile_col(j, i))),
        out_shape=jax.ShapeDtypeStruct((m, n), out_dtype),
        scratch_shapes=scratch,
        compiler_params=_params(2),
        name=name,
    )(*operands, *extras)


def _normalize(x, mu, rstd, g, b):
    return (x - mu) * rstd * g + b


def _store_row_stats(stat_ref, mu, rstd):
    rows = stat_ref.shape[0]
    stat_ref[:, :LANES] = jnp.broadcast_to(mu, (rows, LANES))
    stat_ref[:, LANES:] = jnp.broadcast_to(rstd, (rows, LANES))


def _ln_body(x_ref, g_ref, b_ref, y_ref, *stat_refs):
    x = x_ref[...]
    mu = jnp.mean(x, axis=-1, keepdims=True)
    xc = x - mu
    rstd = lax.rsqrt(jnp.mean(xc * xc, axis=-1, keepdims=True) + LN_EPS)
    y_ref[...] = _normalize(x, mu, rstd, g_ref[...], b_ref[...]).astype(y_ref.dtype)
    if stat_refs:
        _store_row_stats(stat_refs[0], mu, rstd)


def _layer_norm(x, g, b, out_dtype, bm, with_stats):
    m, d = x.shape
    row = pl.BlockSpec((bm, d), lambda i: (i, 0))
    vec = pl.BlockSpec((1, d), lambda i: (0, 0))
    stat = pl.BlockSpec((bm, 2 * LANES), lambda i: (i, 0))
    n_stats = 1 if with_stats else 0
    return pl.pallas_call(
        _ln_body,
        grid=(m // bm,),
        in_specs=[row, vec, vec],
        out_specs=[row] + [stat] * n_stats,
        out_shape=[jax.ShapeDtypeStruct((m, d), out_dtype)] + [jax.ShapeDtypeStruct((m, 2 * LANES), F32)] * n_stats,
        compiler_params=_params(1),
        name="layer_norm",
    )(x, g.reshape(1, d), b.reshape(1, d))


def _rope_tables(pos):
    inv_freq = ROPE_THETA ** (-jnp.arange(ROPE_HALF, dtype=F32) * 2.0 / ROPE_DIM)
    ang = pos.astype(F32)[:, None] * inv_freq[None, :]
    lane = jnp.arange(LANES) % HEAD_DIM
    cos = jnp.cos(ang)[:, lane % ROPE_HALF]
    sin = jnp.sin(ang)[:, lane % ROPE_HALF]
    c = jnp.where(lane < ROPE_DIM, cos, 1.0)
    s_lo = jnp.where(lane < ROPE_HALF, -sin, 0.0)
    s_hi = jnp.where((lane >= ROPE_HALF) & (lane < ROPE_DIM), sin, 0.0)
    return jnp.stack([c, s_lo, s_hi])


def _rope_chunk(x, tab_ref, rs):
    return (x * tab_ref[0, rs, :]
            + pltpu.roll(x, LANES - ROPE_HALF, 1) * tab_ref[1, rs, :]
            + pltpu.roll(x, ROPE_HALF, 1) * tab_ref[2, rs, :])


def _rope_rows(x_ref, w_ref, tab_ref, o_ref, rs, rope_cols, scale):
    acc = jnp.dot(x_ref[rs, :], w_ref[...], preferred_element_type=F32)
    for c in range(acc.shape[1] // LANES):
        blk = acc[:, c * LANES:(c + 1) * LANES]
        if c * LANES < rope_cols:
            blk = _rope_chunk(blk, tab_ref, rs)
        if scale != 1.0:
            blk = blk * scale
        o_ref[rs, c * LANES:(c + 1) * LANES] = blk.astype(o_ref.dtype)


def _rope_compute(x_refs, w_refs, extra_refs, o_ref, *, rope_cols, scale):
    (x_ref,), (w_ref,), (tab_ref,) = x_refs, w_refs, extra_refs
    for rs in _row_chunks(x_ref.shape[0]):
        _rope_rows(x_ref, w_ref, tab_ref, o_ref, rs, rope_cols, scale)


def _plain_compute(x_refs, w_refs, extra_refs, o_ref):
    (x_ref,), (w_ref,) = x_refs, w_refs
    for rs in _row_chunks(x_ref.shape[0]):
        o_ref[rs, :] = jnp.dot(x_ref[rs, :], w_ref[...], preferred_element_type=F32).astype(o_ref.dtype)


def _gate_compute(x_refs, w_refs, extra_refs, o_ref):
    (x_ref,), (w_ref,), (b_ref,) = x_refs, w_refs, extra_refs
    for rs in _row_chunks(x_ref.shape[0]):
        acc = jnp.dot(x_ref[rs, :], w_ref[...], preferred_element_type=F32)
        o_ref[rs, :] = jax.nn.sigmoid(acc + b_ref[...]).astype(o_ref.dtype)


def _rope_spec(tab, bm):
    t_blocks = tab.shape[1] // bm
    return pl.BlockSpec((3, bm, LANES), lambda j, i: (0, _tile_row(j, i) % t_blocks, 0))


def _ln_kv_body(x_ref, g_ref, b_ref, w_ref, tab_ref, h_ref, stat_ref, kv_ref, even_ref, odd_ref, wb_ref):
    s = pl.program_id(0)

    @pl.when(s == 0)
    def _():
        wb_ref[...] = w_ref[...].astype(BF16)
        odd_ref[...] = jnp.zeros(odd_ref.shape, BF16)

    def step(keep_ref, prev_ref):
        _rope_compute([prev_ref], [wb_ref], [tab_ref], kv_ref, rope_cols=KV_WIDTH, scale=1.0)
        x = x_ref[...]
        mu = jnp.mean(x, axis=-1, keepdims=True)
        xc = x - mu
        rstd = lax.rsqrt(jnp.mean(xc * xc, axis=-1, keepdims=True) + LN_EPS)
        hb = _normalize(x, mu, rstd, g_ref[...], b_ref[...]).astype(BF16)
        h_ref[...] = hb
        keep_ref[...] = hb
        _store_row_stats(stat_ref, mu, rstd)

    @pl.when(s % 2 == 0)
    def _():
        step(even_ref, odd_ref)

    @pl.when(s % 2 == 1)
    def _():
        step(odd_ref, even_ref)


def _ln_kv(x, g, b, w_in, col0, tab, bm):
    m, d = x.shape
    n = 2 * KV_WIDTH
    nt = m // bm
    t_blocks = tab.shape[1] // bm
    ln_tile = lambda s: jnp.minimum(s, nt - 1)
    mm_tile = lambda s: jnp.maximum(s - 1, 0)
    row = pl.BlockSpec((bm, d), lambda s: (ln_tile(s), 0))
    vec = pl.BlockSpec((1, d), lambda s: (0, 0))
    stat = pl.BlockSpec((bm, 2 * LANES), lambda s: (ln_tile(s), 0))
    return pl.pallas_call(
        _ln_kv_body,
        grid=(nt + 1,),
        in_specs=[row, vec, vec,
                  pl.BlockSpec((None, d, n), lambda s: (0, 0, col0 // n)),
                  pl.BlockSpec((3, bm, LANES), lambda s: (0, mm_tile(s) % t_blocks, 0))],
        out_specs=[row, stat, pl.BlockSpec((bm, n), lambda s: (mm_tile(s), 0))],
        out_shape=[jax.ShapeDtypeStruct((m, d), BF16), jax.ShapeDtypeStruct((m, 2 * LANES), F32),
                   jax.ShapeDtypeStruct((m, n), BF16)],
        scratch_shapes=[pltpu.VMEM((bm, d), BF16), pltpu.VMEM((bm, d), BF16), pltpu.VMEM((d, n), BF16)],
        compiler_params=_params(1),
        name="ln_in_proj_kv",
    )(x, g.reshape(1, d), b.reshape(1, d), w_in, tab)


_NT = (((1,), (1,)), ((), ()))


HEADS_PER_DOT = 4
HEADS_PER_VREG = LANES // HEAD_DIM
ATTN_BLOCKS_PER_STEP = 4


def _attn_body(sink_ref, q_ref, kvp_ref, kvc_ref, kvm_ref, o_ref):
    i = pl.program_id(1)
    key = lax.broadcasted_iota(jnp.int32, (WINDOW, WINDOW), 0)
    qry = lax.broadcasted_iota(jnp.int32, (WINDOW, WINDOW), 1)
    in_cur = key <= qry
    first_prev_bias = jnp.where(i > 0, 0.0, NEG_INF).astype(F32)
    lane_slot = lax.broadcasted_iota(jnp.int32, (1, LANES), 1) // HEAD_DIM
    meta_pad = jnp.zeros((LANES - N_META, LANES), F32)

    n_band = 2 * WINDOW
    quads = []
    for blk in range(q_ref.shape[0] // WINDOW):
        qrows = slice(blk * WINDOW, (blk + 1) * WINDOW)
        prev = (lambda cols: kvp_ref[:, cols]) if blk == 0 else (
            lambda cols, r=slice((blk - 1) * WINDOW, blk * WINDOW): kvc_ref[r, cols])
        prev_bias = first_prev_bias if blk == 0 else None
        for kvh in range(N_KV_HEADS):
            pair_blk, slot = divmod(kvh, HEADS_PER_VREG)
            kcols = slice(pair_blk * LANES, (pair_blk + 1) * LANES)
            vcols = slice(KV_WIDTH + pair_blk * LANES, KV_WIDTH + (pair_blk + 1) * LANES)

            k2 = jnp.concatenate([prev(kcols), kvc_ref[qrows, kcols], kvm_ref[:, kcols]], axis=0).astype(F32)
            k_own = jnp.where(lane_slot == slot, k2, 0.0)
            k_oth = pltpu.roll(k_own, HEAD_DIM, 1)
            k_own, k_oth = k_own.astype(BF16), k_oth.astype(BF16)

            k_par = [jnp.concatenate([k_own if e == slot else k_oth] * HEADS_PER_VREG, axis=1)
                     for e in range(HEADS_PER_VREG)]
            k_band = jnp.concatenate([kp[:n_band] for kp in k_par], axis=0)
            k_meta = jnp.concatenate([kp[n_band:] for kp in k_par], axis=0)

            v2 = jnp.concatenate([prev(vcols), kvc_ref[qrows, vcols]], axis=0).astype(F32)
            vm2 = jnp.concatenate([kvm_ref[:, vcols].astype(F32), meta_pad], axis=0)
            rows = slice(slot * HEAD_DIM, (slot + 1) * HEAD_DIM)
            v_t = jnp.concatenate([v2.T[rows], vm2.T[rows]], axis=1).astype(BF16)

            for part in range(Q_PER_KV // HEADS_PER_DOT):
                h0 = kvh * Q_PER_KV + part * HEADS_PER_DOT
                xq = q_ref[qrows, h0 * HEAD_DIM:(h0 + HEADS_PER_DOT) * HEAD_DIM]
                q_zero = jnp.zeros((WINDOW, LANES), BF16)
                q_bd = jnp.concatenate([jnp.concatenate([xq[:, :LANES], q_zero], axis=1),
                                        jnp.concatenate([q_zero, xq[:, LANES:]], axis=1)], axis=0)
                st = lax.dot_general(k_band, q_bd, _NT, preferred_element_type=F32)
                stm = lax.dot_general(k_meta, q_bd, _NT, preferred_element_type=F32)
                quads.append((qrows, prev_bias, h0, st, stm, v_t))

    for qrows, prev_bias, h0, st, stm, v_t in quads:
        outs = []
        for pair in range(HEADS_PER_DOT // HEADS_PER_VREG):
            qs = slice(pair * WINDOW, (pair + 1) * WINDOW)
            probs, inv = [], []
            for e in range(HEADS_PER_VREG):
                s_prev = st[e * n_band:e * n_band + WINDOW, qs]
                if prev_bias is not None:
                    s_prev = s_prev + prev_bias
                s_cur = st[e * n_band + WINDOW:(e + 1) * n_band, qs]
                s = jnp.where(in_cur, s_cur, s_prev)
                sm = stm[e * N_META:(e + 1) * N_META, qs]
                sink = sink_ref[h0 + pair * HEADS_PER_VREG + e]
                mx = jnp.maximum(jnp.maximum(jnp.max(s, axis=0, keepdims=True),
                                             jnp.max(sm, axis=0, keepdims=True)), sink)
                p = jnp.exp(s - mx)
                pm = jnp.exp(sm - mx)
                denom = (jnp.sum(p, axis=0, keepdims=True) + jnp.sum(pm, axis=0, keepdims=True)
                         + jnp.exp(sink - mx))
                probs.append(jnp.concatenate(
                    [jnp.where(in_cur, 0.0, p), jnp.where(in_cur, p, 0.0), pm, meta_pad], axis=0))
                inv.append(1.0 / denom)
            p_pair = jnp.concatenate(probs, axis=1).astype(BF16)
            o_pair = jnp.dot(v_t, p_pair, preferred_element_type=F32)
            outs += [o_pair[:, e * WINDOW:(e + 1) * WINDOW] * inv[e] for e in range(HEADS_PER_VREG)]
        qcols = slice(h0 * HEAD_DIM, (h0 + HEADS_PER_DOT) * HEAD_DIM)
        o_ref[qrows, qcols] = jnp.concatenate(outs, axis=0).T.astype(o_ref.dtype)


def _attention(q, kv, kv_meta, sinks, batch, seq):
    bq = ATTN_BLOCKS_PER_STEP * WINDOW
    nb = seq // bq
    halo_per_blk = ATTN_BLOCKS_PER_STEP
    return pl.pallas_call(
        _attn_body,
        grid=(batch, nb),
        in_specs=[pl.BlockSpec(memory_space=pltpu.SMEM),
                  pl.BlockSpec((bq, ATTN_WIDTH), lambda b, i: (b * nb + i, 0)),
                  pl.BlockSpec((WINDOW, 2 * KV_WIDTH),
                               lambda b, i: (jnp.maximum((b * nb + i) * halo_per_blk - 1, 0), 0)),
                  pl.BlockSpec((bq, 2 * KV_WIDTH), lambda b, i: (b * nb + i, 0)),
                  pl.BlockSpec((N_META, 2 * KV_WIDTH), lambda b, i: (0, 0))],
        out_specs=pl.BlockSpec((bq, ATTN_WIDTH), lambda b, i: (b * nb + i, 0)),
        out_shape=jax.ShapeDtypeStruct((batch * seq, ATTN_WIDTH), BF16),
        compiler_params=_params(2),
        name="swa_attention",
    )(sinks, q, kv, kv, kv_meta)


POOL_HALO = 16
POOL_PAD = 16
SUBLANES = 8


def _pool_body(up_ref, uc_ref, um_ref, w_ref, sc_ref, o_ref, ext_ref, sa_ref, sb_ref, *, bt):
    i = pl.program_id(1)
    base = POOL_PAD + POOL_HALO
    end = base + bt
    ext_ref[0:POOL_PAD, :] = jnp.zeros((POOL_PAD, POOL_WIDTH), F32)
    ext_ref[POOL_PAD:base, :] = jnp.where(i == 0, um_ref[...], up_ref[...])
    ext_ref[base:, :] = uc_ref[...]
    for g, w in enumerate(POOL_WINDOWS):
        cs = slice(g * POOL_GROUP_WIDTH, (g + 1) * POOL_GROUP_WIDTH)
        src, shift = ext_ref, 1
        for level in range(g):
            start = SUBLANES * (level + 1)
            dst = (sa_ref, sb_ref)[level % 2]
            dst[start:end, cs] = src[start:end, cs] + src[start - shift:end - shift, cs]
            src, shift = dst, 2 * shift
        win = src[base:end, cs] + src[base - shift:end - shift, cs]
        pooled = win / float(w) - uc_ref[:, cs]
        mixed = jnp.dot(pooled.astype(BF16), w_ref[g], preferred_element_type=F32)
        o_ref[:, cs] = (mixed * sc_ref[:, cs]).astype(o_ref.dtype)


def _pool(u, u_meta, w_grp, scale, batch, seq, bt):
    assert all(w == 2 << g for g, w in enumerate(POOL_WINDOWS))
    nb = seq // bt
    halo_per_blk = bt // POOL_HALO
    ext = pltpu.VMEM((POOL_PAD + POOL_HALO + bt, POOL_WIDTH), F32)
    return pl.pallas_call(
        functools.partial(_pool_body, bt=bt),
        grid=(batch, nb),
        in_specs=[pl.BlockSpec((POOL_HALO, POOL_WIDTH),
                               lambda b, i: (jnp.maximum((b * nb + i) * halo_per_blk - 1, 0), 0)),
                  pl.BlockSpec((bt, POOL_WIDTH), lambda b, i: (b * nb + i, 0)),
                  pl.BlockSpec((N_META, POOL_WIDTH), lambda b, i: (0, 0)),
                  pl.BlockSpec((len(POOL_WINDOWS), POOL_GROUP_WIDTH, POOL_GROUP_WIDTH), lambda b, i: (0, 0, 0)),
                  pl.BlockSpec((1, POOL_WIDTH), lambda b, i: (0, 0))],
        out_specs=pl.BlockSpec((bt, POOL_WIDTH), lambda b, i: (b * nb + i, 0)),
        out_shape=jax.ShapeDtypeStruct((batch * seq, POOL_WIDTH), BF16),
        scratch_shapes=[ext, ext, ext],
        compiler_params=_params(2),
        name="multiscale_pool",
    )(u, u, u_meta, w_grp, scale)


def _up_compute(x_refs, w_refs, extra_refs, o_ref):
    (a_ref, p_ref), (wa_ref, wp_ref), (ga_ref, gb_ref) = x_refs, w_refs, extra_refs
    for rs in _row_chunks(a_ref.shape[0]):
        a = jnp.dot(a_ref[rs, :], wa_ref[...], preferred_element_type=F32)
        p = jnp.dot(p_ref[rs, :], wp_ref[...], preferred_element_type=F32)
        o_ref[rs, :] = (ga_ref[rs, :].astype(F32) * a + gb_ref[rs, :].astype(F32) * p).astype(o_ref.dtype)


def _resid_compute(x_refs, w_refs, extra_refs, o_ref):
    (x_ref,), (w_ref,), (r_ref, stat_ref, gb_ref) = x_refs, w_refs, extra_refs
    for rs in _row_chunks(x_ref.shape[0]):
        acc = jnp.dot(x_ref[rs, :], w_ref[...], preferred_element_type=F32)
        mu, rstd = stat_ref[rs, :LANES], stat_ref[rs, LANES:]
        for c in range(acc.shape[1] // LANES):
            cs = slice(c * LANES, (c + 1) * LANES)
            h = _normalize(r_ref[rs, cs], mu, rstd, gb_ref[0:1, cs], gb_ref[1:2, cs])
            o_ref[rs, cs] = DN_ALPHA * h + acc[:, cs]


def _resid_specs(bm, bn, row, col):
    return [pl.BlockSpec((bm, bn), lambda j, i: (row(j, i), col(j, i))),
            pl.BlockSpec((bm, 2 * LANES), lambda j, i: (row(j, i), 0)),
            pl.BlockSpec((2, bn), lambda j, i: (0, col(j, i)))]


def _swiglu_compute(x_refs, w_refs, extra_refs, o_ref):
    (x_ref,), (wg_ref, wu_ref) = x_refs, w_refs
    for rs in _row_chunks(x_ref.shape[0]):
        x = x_ref[rs, :]
        g = jnp.dot(x, wg_ref[...], preferred_element_type=F32)
        u = jnp.dot(x, wu_ref[...], preferred_element_type=F32)
        o_ref[rs, :] = (jax.nn.silu(g) * u).astype(o_ref.dtype)


def kernel(x, meta_tokens, ln_in_g, ln_in_b, w_in, b_gate, attn_sinks, w_attn_up, w_pool_grp, pool_scale,
           w_pool_up, w_out, ln1_g, ln1_b, w_ffn_in, w_ffn_down, ln2_g, ln2_b):
    batch, seq, d = x.shape
    m = batch * seq
    x2 = x.reshape(m, d)
    q0, k0, _, u0, g0 = IN_SPLITS

    tab_x = _rope_tables(jnp.arange(N_META, N_META + seq))
    tab_m = _rope_tables(jnp.arange(N_META))

    h0b, stat0, kv = _ln_kv(x2, ln_in_g, ln_in_b, w_in, k0, tab_x, LN_KV_ROW_TILE)
    (hmb,) = _layer_norm(meta_tokens, ln_in_g, ln_in_b, BF16, N_META, False)

    rope_q = functools.partial(_rope_compute, rope_cols=ATTN_WIDTH, scale=ATTN_SCALE)
    rope_kv = functools.partial(_rope_compute, rope_cols=KV_WIDTH, scale=1.0)
    wide, narrow = COL_TILE, STAGE_COLS
    q = _staged_matmul(rope_q, [h0b], [(w_in, q0)], [tab_x], [_rope_spec(tab_x, ROW_TILE)], BF16, ATTN_WIDTH,
                       ROW_TILE, wide, "proj_q")
    kv_meta = _staged_matmul(rope_kv, [hmb], [(w_in, k0)], [tab_m], [_rope_spec(tab_m, N_META)], BF16,
                             2 * KV_WIDTH, N_META, narrow, "proj_kv_meta")
    u = _staged_matmul(_plain_compute, [h0b], [(w_in, u0)], [], [], F32, POOL_WIDTH, ROW_TILE, wide, "proj_u")
    u_meta = _staged_matmul(_plain_compute, [hmb], [(w_in, u0)], [], [], F32, POOL_WIDTH, N_META, narrow,
                            "proj_u_meta")
    gates = _staged_matmul(_gate_compute, [h0b], [(w_in, g0)], [b_gate[0].reshape(1, 2 * d)],
                           [pl.BlockSpec((1, wide), lambda j, i: (0, _tile_col(j, i)))], BF16, 2 * d,
                           ROW_TILE, wide, "proj_gate")

    attn = _attention(q, kv, kv_meta, attn_sinks[0], batch, seq)
    pool = _pool(u, u_meta, w_pool_grp[0].astype(BF16), pool_scale[0].reshape(1, POOL_WIDTH), batch, seq,
                 POOL_ROW_TILE)

    gate_b0 = d // wide
    mixed = _staged_matmul(
        _up_compute, [attn, pool], [(w_attn_up, 0), (w_pool_up, 0)], [gates, gates],
        [pl.BlockSpec((ROW_TILE, wide), lambda j, i: (_tile_row(j, i), _tile_col(j, i))),
         pl.BlockSpec((ROW_TILE, wide), lambda j, i: (_tile_row(j, i), _tile_col(j, i) + gate_b0))],
        BF16, d, ROW_TILE, wide, "up_merge")
    y1 = _staged_matmul(
        _resid_compute, [mixed], [(w_out, 0)], [x2, stat0, jnp.stack([ln_in_g, ln_in_b])],
        _resid_specs(ROW_TILE, wide, _tile_row, _tile_col), F32, d, ROW_TILE, wide, "out_proj")
    h1b, stat1 = _layer_norm(y1, ln1_g[0], ln1_b[0], BF16, LN_ROW_TILE, True)

    act = _staged_matmul(_swiglu_compute, [h1b], [(w_ffn_in, 0), (w_ffn_in, D_FF)], [], [], BF16, D_FF,
                         FFN_ROW_TILE, FFN_COL_TILE, "swiglu_in")
    y2 = _staged_matmul(
        _resid_compute, [act], [(w_ffn_down, 0)], [y1, stat1, jnp.stack([ln1_g[0], ln1_b[0]])],
        _resid_specs(DOWN_ROW_TILE, DOWN_COL_TILE, _tile_row, _tile_col), F32, d, DOWN_ROW_TILE, DOWN_COL_TILE,
        "ffn_down")
    (out,) = _layer_norm(y2, ln2_g[0], ln2_b[0], F32, LN_ROW_TILE, False)
    return out.reshape(batch, seq, d)
```

```python
import functools

import jax
import jax.numpy as jnp
from jax import lax
from jax.experimental import pallas as pl
from jax.experimental.pallas import tpu as pltpu

F32 = jnp.float32
BF16 = jnp.bfloat16

D_MODEL = 4096
N_META = 16
HEAD_DIM = 64
N_Q_HEADS = 32
N_KV_HEADS = 4
Q_PER_KV = N_Q_HEADS // N_KV_HEADS
ATTN_WIDTH = N_Q_HEADS * HEAD_DIM
KV_WIDTH = N_KV_HEADS * HEAD_DIM
WINDOW = 128
ATTN_SCALE = HEAD_DIM ** -0.5
ROPE_DIM = HEAD_DIM // 4
ROPE_HALF = ROPE_DIM // 2
ROPE_THETA = 500000.0
NEG_INF = -1e30
POOL_WINDOWS = (2, 4, 8, 16)
POOL_WIDTH = D_MODEL // 2
POOL_GROUP_WIDTH = POOL_WIDTH // len(POOL_WINDOWS)
IN_SPLITS = (0, ATTN_WIDTH, ATTN_WIDTH + KV_WIDTH, ATTN_WIDTH + 2 * KV_WIDTH,
             ATTN_WIDTH + 2 * KV_WIDTH + POOL_WIDTH)
D_FF = -(-8 * D_MODEL // 768) * 256
DEPTH = 1
DN_ALPHA = (2 * DEPTH) ** 0.25
LN_EPS = 1e-5

LANES = 128
VMEM_LIMIT = 56 * 1024 * 1024

ROW_TILE = 1024
COL_TILE = 1024
STAGE_COLS = 512
MAX_STAGE_CHUNKS = 16
FFN_COL_TILE = 256
FFN_ROW_TILE = 2048
DOWN_ROW_TILE = 512
DOWN_COL_TILE = 512
LN_ROW_TILE = 512
LN_KV_ROW_TILE = 512
POOL_ROW_TILE = 512


FUSED_NORM_VMEM_LIMIT = 60 * 1024 * 1024


def _params(n_grid, vmem_limit=VMEM_LIMIT):
    return pltpu.CompilerParams(dimension_semantics=("arbitrary",) * n_grid, vmem_limit_bytes=vmem_limit)


MM_ROW_CHUNK = 256


def _row_chunks(bm):
    chunk = min(bm, MM_ROW_CHUNK)
    return [slice(r, r + chunk) for r in range(0, bm, chunk)]


def _tile_row(j, i):
    return jnp.where(j > 0, i, 0)


def _tile_col(j, i):
    return jnp.maximum(j - 1, 0)


def _staged_body(*refs, n_x, n_w, sub, steps_per_chunk, compute):
    x_refs = refs[:n_x]
    w_refs = refs[n_x:n_x + n_w * sub]
    extra_refs = refs[n_x + n_w * sub:len(refs) - n_w - 1]
    o_ref = refs[len(refs) - n_w - 1]
    wb_refs = refs[len(refs) - n_w:]
    j, i = pl.program_id(0), pl.program_id(1)

    def stage():
        for t, wb_ref in enumerate(wb_refs):
            for h in range(sub):
                w_ref = w_refs[t * sub + h]
                kc = w_ref.shape[0]
                rows = pl.ds(pl.multiple_of((i // steps_per_chunk) * kc, kc), kc)
                sw = w_ref.shape[1]
                wb_ref[j % 2, rows, h * sw:(h + 1) * sw] = w_ref[...].astype(BF16)

    @pl.when(j == 0)
    def _():
        stage()

    @pl.when(j > 0)
    def _():
        stage()
        compute(x_refs, [wb_ref.at[(j + 1) % 2] for wb_ref in wb_refs], extra_refs, o_ref)


def _staged_matmul(compute, xs, ws, extras, extra_specs, out_dtype, n, bm, bn, name):
    m = xs[0].shape[0]
    ni, nj = m // bm, n // bn
    aligned = all(col0 % bn == 0 for _, col0 in ws)
    sw = bn if aligned else min(bn, STAGE_COLS)
    sub = bn // sw
    n_chunks = min(ni, MAX_STAGE_CHUNKS)
    steps_per_chunk = ni // n_chunks
    in_specs = [pl.BlockSpec((bm, x.shape[1]), lambda j, i: (_tile_row(j, i), 0)) for x in xs]
    operands = list(xs)
    scratch = []
    for w, col0 in ws:
        k = w.shape[1]
        for h in range(sub):
            first = col0 // sw + h
            in_specs.append(pl.BlockSpec(
                (None, k // n_chunks, sw),
                lambda j, i, first=first: (0, i // steps_per_chunk, first + sub * jnp.minimum(j, nj - 1))))
            operands.append(w)
        scratch.append(pltpu.VMEM((2, k, bn), BF16))
    return pl.pallas_call(
        functools.partial(_staged_body, n_x=len(xs), n_w=len(ws), sub=sub, steps_per_chunk=steps_per_chunk,
                          compute=compute),
        grid=(nj + 1, ni),
        in_specs=in_specs + extra_specs,
        out_specs=pl.BlockSpec((bm, bn), lambda j, i: (_tile_row(j, i), _tile_col(j, i))),
        out_shape=jax.ShapeDtypeStruct((m, n), out_dtype),
        scratch_shapes=scratch,
        compiler_params=_params(2),
        name=name,
    )(*operands, *extras)


def _normalize(x, mu, rstd, g, b):
    return (x - mu) * rstd * g + b


def _store_row_stats(stat_ref, mu, rstd):
    rows = stat_ref.shape[0]
    stat_ref[:, :LANES] = jnp.broadcast_to(mu, (rows, LANES))
    stat_ref[:, LANES:] = jnp.broadcast_to(rstd, (rows, LANES))


def _ln_body(x_ref, g_ref, b_ref, y_ref, *stat_refs):
    x = x_ref[...]
    mu = jnp.mean(x, axis=-1, keepdims=True)
    xc = x - mu
    rstd = lax.rsqrt(jnp.mean(xc * xc, axis=-1, keepdims=True) + LN_EPS)
    y_ref[...] = _normalize(x, mu, rstd, g_ref[...], b_ref[...]).astype(y_ref.dtype)
    if stat_refs:
        _store_row_stats(stat_refs[0], mu, rstd)


def _layer_norm(x, g, b, out_dtype, bm, with_stats):
    m, d = x.shape
    row = pl.BlockSpec((bm, d), lambda i: (i, 0))
    vec = pl.BlockSpec((1, d), lambda i: (0, 0))
    stat = pl.BlockSpec((bm, 2 * LANES), lambda i: (i, 0))
    n_stats = 1 if with_stats else 0
    return pl.pallas_call(
        _ln_body,
        grid=(m // bm,),
        in_specs=[row, vec, vec],
        out_specs=[row] + [stat] * n_stats,
        out_shape=[jax.ShapeDtypeStruct((m, d), out_dtype)] + [jax.ShapeDtypeStruct((m, 2 * LANES), F32)] * n_stats,
        compiler_params=_params(1),
        name="layer_norm",
    )(x, g.reshape(1, d), b.reshape(1, d))


def _rope_tables(pos):
    inv_freq = ROPE_THETA ** (-jnp.arange(ROPE_HALF, dtype=F32) * 2.0 / ROPE_DIM)
    ang = pos.astype(F32)[:, None] * inv_freq[None, :]
    lane = jnp.arange(LANES) % HEAD_DIM
    cos = jnp.cos(ang)[:, lane % ROPE_HALF]
    sin = jnp.sin(ang)[:, lane % ROPE_HALF]
    c = jnp.where(lane < ROPE_DIM, cos, 1.0)
    s_lo = jnp.where(lane < ROPE_HALF, -sin, 0.0)
    s_hi = jnp.where((lane >= ROPE_HALF) & (lane < ROPE_DIM), sin, 0.0)
    return jnp.stack([c, s_lo, s_hi])


def _rope_chunk(x, tab_ref, rs):
    return (x * tab_ref[0, rs, :]
            + pltpu.roll(x, LANES - ROPE_HALF, 1) * tab_ref[1, rs, :]
            + pltpu.roll(x, ROPE_HALF, 1) * tab_ref[2, rs, :])


def _rope_rows(x_ref, w_ref, tab_ref, o_ref, rs, rope_cols, scale):
    acc = jnp.dot(x_ref[rs, :], w_ref[...], preferred_element_type=F32)
    for c in range(acc.shape[1] // LANES):
        blk = acc[:, c * LANES:(c + 1) * LANES]
        if c * LANES < rope_cols:
            blk = _rope_chunk(blk, tab_ref, rs)
        if scale != 1.0:
            blk = blk * scale
        o_ref[rs, c * LANES:(c + 1) * LANES] = blk.astype(o_ref.dtype)


def _rope_compute(x_refs, w_refs, extra_refs, o_ref, *, rope_cols, scale):
    (x_ref,), (w_ref,), (tab_ref,) = x_refs, w_refs, extra_refs
    for rs in _row_chunks(x_ref.shape[0]):
        _rope_rows(x_ref, w_ref, tab_ref, o_ref, rs, rope_cols, scale)


def _plain_compute(x_refs, w_refs, extra_refs, o_ref):
    (x_ref,), (w_ref,) = x_refs, w_refs
    for rs in _row_chunks(x_ref.shape[0]):
        o_ref[rs, :] = jnp.dot(x_ref[rs, :], w_ref[...], preferred_element_type=F32).astype(o_ref.dtype)


def _gate_compute(x_refs, w_refs, extra_refs, o_ref):
    (x_ref,), (w_ref,), (b_ref,) = x_refs, w_refs, extra_refs
    for rs in _row_chunks(x_ref.shape[0]):
        acc = jnp.dot(x_ref[rs, :], w_ref[...], preferred_element_type=F32)
        o_ref[rs, :] = jax.nn.sigmoid(acc + b_ref[...]).astype(o_ref.dtype)


def _rope_spec(tab, bm):
    t_blocks = tab.shape[1] // bm
    return pl.BlockSpec((3, bm, LANES), lambda j, i: (0, _tile_row(j, i) % t_blocks, 0))


def _ln_kv_body(x_ref, g_ref, b_ref, w_ref, tab_ref, h_ref, stat_ref, kv_ref, even_ref, odd_ref, wb_ref):
    s = pl.program_id(0)

    @pl.when(s == 0)
    def _():
        wb_ref[...] = w_ref[...].astype(BF16)
        odd_ref[...] = jnp.zeros(odd_ref.shape, BF16)

    def step(keep_ref, prev_ref):
        _rope_compute([prev_ref], [wb_ref], [tab_ref], kv_ref, rope_cols=KV_WIDTH, scale=1.0)
        x = x_ref[...]
        mu = jnp.mean(x, axis=-1, keepdims=True)
        xc = x - mu
        rstd = lax.rsqrt(jnp.mean(xc * xc, axis=-1, keepdims=True) + LN_EPS)
        hb = _normalize(x, mu, rstd, g_ref[...], b_ref[...]).astype(BF16)
        h_ref[...] = hb
        keep_ref[...] = hb
        _store_row_stats(stat_ref, mu, rstd)

    @pl.when(s % 2 == 0)
    def _():
        step(even_ref, odd_ref)

    @pl.when(s % 2 == 1)
    def _():
        step(odd_ref, even_ref)


def _ln_kv(x, g, b, w_in, col0, tab, bm):
    m, d = x.shape
    n = 2 * KV_WIDTH
    nt = m // bm
    t_blocks = tab.shape[1] // bm
    ln_tile = lambda s: jnp.minimum(s, nt - 1)
    mm_tile = lambda s: jnp.maximum(s - 1, 0)
    row = pl.BlockSpec((bm, d), lambda s: (ln_tile(s), 0))
    vec = pl.BlockSpec((1, d), lambda s: (0, 0))
    stat = pl.BlockSpec((bm, 2 * LANES), lambda s: (ln_tile(s), 0))
    return pl.pallas_call(
        _ln_kv_body,
        grid=(nt + 1,),
        in_specs=[row, vec, vec,
                  pl.BlockSpec((None, d, n), lambda s: (0, 0, col0 // n), pipeline_mode=pl.Buffered(1)),
                  pl.BlockSpec((3, bm, LANES), lambda s: (0, mm_tile(s) % t_blocks, 0))],
        out_specs=[row, stat, pl.BlockSpec((bm, n), lambda s: (mm_tile(s), 0))],
        out_shape=[jax.ShapeDtypeStruct((m, d), BF16), jax.ShapeDtypeStruct((m, 2 * LANES), F32),
                   jax.ShapeDtypeStruct((m, n), BF16)],
        scratch_shapes=[pltpu.VMEM((bm, d), BF16), pltpu.VMEM((bm, d), BF16), pltpu.VMEM((d, n), BF16)],
        compiler_params=_params(1, FUSED_NORM_VMEM_LIMIT),
        name="ln_in_proj_kv",
    )(x, g.reshape(1, d), b.reshape(1, d), w_in, tab)


_NT = (((1,), (1,)), ((), ()))


HEADS_PER_DOT = 4
HEADS_PER_VREG = LANES // HEAD_DIM
ATTN_BLOCKS_PER_STEP = 4


def _attn_body(sink_ref, q_ref, kvp_ref, kvc_ref, kvm_ref, o_ref):
    i = pl.program_id(1)
    key = lax.broadcasted_iota(jnp.int32, (WINDOW, WINDOW), 0)
    qry = lax.broadcasted_iota(jnp.int32, (WINDOW, WINDOW), 1)
    in_cur = key <= qry
    first_prev_bias = jnp.where(i > 0, 0.0, NEG_INF).astype(F32)
    lane_slot = lax.broadcasted_iota(jnp.int32, (1, LANES), 1) // HEAD_DIM
    meta_pad = jnp.zeros((LANES - N_META, LANES), F32)

    n_band = 2 * WINDOW
    quads = []
    for blk in range(q_ref.shape[0] // WINDOW):
        qrows = slice(blk * WINDOW, (blk + 1) * WINDOW)
        prev = (lambda cols: kvp_ref[:, cols]) if blk == 0 else (
            lambda cols, r=slice((blk - 1) * WINDOW, blk * WINDOW): kvc_ref[r, cols])
        prev_bias = first_prev_bias if blk == 0 else None
        for kvh in range(N_KV_HEADS):
            pair_blk, slot = divmod(kvh, HEADS_PER_VREG)
            kcols = slice(pair_blk * LANES, (pair_blk + 1) * LANES)
            vcols = slice(KV_WIDTH + pair_blk * LANES, KV_WIDTH + (pair_blk + 1) * LANES)

            k2 = jnp.concatenate([prev(kcols), kvc_ref[qrows, kcols], kvm_ref[:, kcols]], axis=0).astype(F32)
            k_own = jnp.where(lane_slot == slot, k2, 0.0)
            k_oth = pltpu.roll(k_own, HEAD_DIM, 1)
            k_own, k_oth = k_own.astype(BF16), k_oth.astype(BF16)

            k_par = [jnp.concatenate([k_own if e == slot else k_oth] * HEADS_PER_VREG, axis=1)
                     for e in range(HEADS_PER_VREG)]
            k_band = jnp.concatenate([kp[:n_band] for kp in k_par], axis=0)
            k_meta = jnp.concatenate([kp[n_band:] for kp in k_par], axis=0)

            v2 = jnp.concatenate([prev(vcols), kvc_ref[qrows, vcols]], axis=0).astype(F32)
            vm2 = jnp.concatenate([kvm_ref[:, vcols].astype(F32), meta_pad], axis=0)
            rows = slice(slot * HEAD_DIM, (slot + 1) * HEAD_DIM)
            v_t = jnp.concatenate([v2.T[rows], vm2.T[rows]], axis=1).astype(BF16)

            for part in range(Q_PER_KV // HEADS_PER_DOT):
                h0 = kvh * Q_PER_KV + part * HEADS_PER_DOT
                xq = q_ref[qrows, h0 * HEAD_DIM:(h0 + HEADS_PER_DOT) * HEAD_DIM]
                q_zero = jnp.zeros((WINDOW, LANES), BF16)
                q_bd = jnp.concatenate([jnp.concatenate([xq[:, :LANES], q_zero], axis=1),
                                        jnp.concatenate([q_zero, xq[:, LANES:]], axis=1)], axis=0)
                st = lax.dot_general(k_band, q_bd, _NT, preferred_element_type=F32)
                stm = lax.dot_general(k_meta, q_bd, _NT, preferred_element_type=F32)
                quads.append((qrows, prev_bias, h0, st, stm, v_t))

    for qrows, prev_bias, h0, st, stm, v_t in quads:
        outs = []
        for pair in range(HEADS_PER_DOT // HEADS_PER_VREG):
            qs = slice(pair * WINDOW, (pair + 1) * WINDOW)
            probs, inv = [], []
            for e in range(HEADS_PER_VREG):
                s_prev = st[e * n_band:e * n_band + WINDOW, qs]
                if prev_bias is not None:
                    s_prev = s_prev + prev_bias
                s_cur = st[e * n_band + WINDOW:(e + 1) * n_band, qs]
                s = jnp.where(in_cur, s_cur, s_prev)
                sm = stm[e * N_META:(e + 1) * N_META, qs]
                sink = sink_ref[h0 + pair * HEADS_PER_VREG + e]
                mx = jnp.maximum(jnp.maximum(jnp.max(s, axis=0, keepdims=True),
                                             jnp.max(sm, axis=0, keepdims=True)), sink)
                p = jnp.exp(s - mx)
                pm = jnp.exp(sm - mx)
                denom = (jnp.sum(p, axis=0, keepdims=True) + jnp.sum(pm, axis=0, keepdims=True)
                         + jnp.exp(sink - mx))
                probs.append(jnp.concatenate(
                    [jnp.where(in_cur, 0.0, p), jnp.where(in_cur, p, 0.0), pm, meta_pad], axis=0))
                inv.append(1.0 / denom)
            p_pair = jnp.concatenate(probs, axis=1).astype(BF16)
            o_pair = jnp.dot(v_t, p_pair, preferred_element_type=F32)
            outs += [o_pair[:, e * WINDOW:(e + 1) * WINDOW] * inv[e] for e in range(HEADS_PER_VREG)]
        qcols = slice(h0 * HEAD_DIM, (h0 + HEADS_PER_DOT) * HEAD_DIM)
        o_ref[qrows, qcols] = jnp.concatenate(outs, axis=0).T.astype(o_ref.dtype)


def _attention(q, kv, kv_meta, sinks, batch, seq):
    bq = ATTN_BLOCKS_PER_STEP * WINDOW
    nb = seq // bq
    halo_per_blk = ATTN_BLOCKS_PER_STEP
    return pl.pallas_call(
        _attn_body,
        grid=(batch, nb),
        in_specs=[pl.BlockSpec(memory_space=pltpu.SMEM),
                  pl.BlockSpec((bq, ATTN_WIDTH), lambda b, i: (b * nb + i, 0)),
                  pl.BlockSpec((WINDOW, 2 * KV_WIDTH),
                               lambda b, i: (jnp.maximum((b * nb + i) * halo_per_blk - 1, 0), 0)),
                  pl.BlockSpec((bq, 2 * KV_WIDTH), lambda b, i: (b * nb + i, 0)),
                  pl.BlockSpec((N_META, 2 * KV_WIDTH), lambda b, i: (0, 0))],
        out_specs=pl.BlockSpec((bq, ATTN_WIDTH), lambda b, i: (b * nb + i, 0)),
        out_shape=jax.ShapeDtypeStruct((batch * seq, ATTN_WIDTH), BF16),
        compiler_params=_params(2),
        name="swa_attention",
    )(sinks, q, kv, kv, kv_meta)


POOL_HALO = 16
POOL_PAD = 16
SUBLANES = 8


def _pool_body(up_ref, uc_ref, um_ref, w_ref, sc_ref, o_ref, ext_ref, sa_ref, sb_ref, *, bt):
    i = pl.program_id(1)
    base = POOL_PAD + POOL_HALO
    end = base + bt
    ext_ref[0:POOL_PAD, :] = jnp.zeros((POOL_PAD, POOL_WIDTH), F32)
    ext_ref[POOL_PAD:base, :] = jnp.where(i == 0, um_ref[...], up_ref[...])
    ext_ref[base:, :] = uc_ref[...]
    for g, w in enumerate(POOL_WINDOWS):
        cs = slice(g * POOL_GROUP_WIDTH, (g + 1) * POOL_GROUP_WIDTH)
        src, shift = ext_ref, 1
        for level in range(g):
            start = SUBLANES * (level + 1)
            dst = (sa_ref, sb_ref)[level % 2]
            dst[start:end, cs] = src[start:end, cs] + src[start - shift:end - shift, cs]
            src, shift = dst, 2 * shift
        win = src[base:end, cs] + src[base - shift:end - shift, cs]
        pooled = win / float(w) - uc_ref[:, cs]
        mixed = jnp.dot(pooled.astype(BF16), w_ref[g], preferred_element_type=F32)
        o_ref[:, cs] = (mixed * sc_ref[:, cs]).astype(o_ref.dtype)


def _pool(u, u_meta, w_grp, scale, batch, seq, bt):
    assert all(w == 2 << g for g, w in enumerate(POOL_WINDOWS))
    nb = seq // bt
    halo_per_blk = bt // POOL_HALO
    ext = pltpu.VMEM((POOL_PAD + POOL_HALO + bt, POOL_WIDTH), F32)
    return pl.pallas_call(
        functools.partial(_pool_body, bt=bt),
        grid=(batch, nb),
        in_specs=[pl.BlockSpec((POOL_HALO, POOL_WIDTH),
                               lambda b, i: (jnp.maximum((b * nb + i) * halo_per_blk - 1, 0), 0)),
                  pl.BlockSpec((bt, POOL_WIDTH), lambda b, i: (b * nb + i, 0)),
                  pl.BlockSpec((N_META, POOL_WIDTH), lambda b, i: (0, 0)),
                  pl.BlockSpec((len(POOL_WINDOWS), POOL_GROUP_WIDTH, POOL_GROUP_WIDTH), lambda b, i: (0, 0, 0)),
                  pl.BlockSpec((1, POOL_WIDTH), lambda b, i: (0, 0))],
        out_specs=pl.BlockSpec((bt, POOL_WIDTH), lambda b, i: (b * nb + i, 0)),
        out_shape=jax.ShapeDtypeStruct((batch * seq, POOL_WIDTH), BF16),
        scratch_shapes=[ext, ext, ext],
        compiler_params=_params(2),
        name="multiscale_pool",
    )(u, u, u_meta, w_grp, scale)


def _up_compute(x_refs, w_refs, extra_refs, o_ref):
    (a_ref, p_ref), (wa_ref, wp_ref), (ga_ref, gb_ref) = x_refs, w_refs, extra_refs
    for rs in _row_chunks(a_ref.shape[0]):
        a = jnp.dot(a_ref[rs, :], wa_ref[...], preferred_element_type=F32)
        p = jnp.dot(p_ref[rs, :], wp_ref[...], preferred_element_type=F32)
        o_ref[rs, :] = (ga_ref[rs, :].astype(F32) * a + gb_ref[rs, :].astype(F32) * p).astype(o_ref.dtype)


def _resid_compute(x_refs, w_refs, extra_refs, o_ref):
    (x_ref,), (w_ref,), (r_ref, stat_ref, gb_ref) = x_refs, w_refs, extra_refs
    for rs in _row_chunks(x_ref.shape[0]):
        acc = jnp.dot(x_ref[rs, :], w_ref[...], preferred_element_type=F32)
        mu, rstd = stat_ref[rs, :LANES], stat_ref[rs, LANES:]
        for c in range(acc.shape[1] // LANES):
            cs = slice(c * LANES, (c + 1) * LANES)
            h = _normalize(r_ref[rs, cs], mu, rstd, gb_ref[0:1, cs], gb_ref[1:2, cs])
            o_ref[rs, cs] = DN_ALPHA * h + acc[:, cs]


def _resid_specs(bm, bn, row, col):
    return [pl.BlockSpec((bm, bn), lambda j, i: (row(j, i), col(j, i))),
            pl.BlockSpec((bm, 2 * LANES), lambda j, i: (row(j, i), 0)),
            pl.BlockSpec((2, bn), lambda j, i: (0, col(j, i)))]


def _swiglu_compute(x_refs, w_refs, extra_refs, o_ref):
    (x_ref,), (wg_ref, wu_ref) = x_refs, w_refs
    for rs in _row_chunks(x_ref.shape[0]):
        x = x_ref[rs, :]
        g = jnp.dot(x, wg_ref[...], preferred_element_type=F32)
        u = jnp.dot(x, wu_ref[...], preferred_element_type=F32)
        o_ref[rs, :] = (jax.nn.silu(g) * u).astype(o_ref.dtype)


def kernel(x, meta_tokens, ln_in_g, ln_in_b, w_in, b_gate, attn_sinks, w_attn_up, w_pool_grp, pool_scale,
           w_pool_up, w_out, ln1_g, ln1_b, w_ffn_in, w_ffn_down, ln2_g, ln2_b):
    batch, seq, d = x.shape
    m = batch * seq
    x2 = x.reshape(m, d)
    q0, k0, _, u0, g0 = IN_SPLITS

    tab_x = _rope_tables(jnp.arange(N_META, N_META + seq))
    tab_m = _rope_tables(jnp.arange(N_META))

    h0b, stat0, kv = _ln_kv(x2, ln_in_g, ln_in_b, w_in, k0, tab_x, LN_KV_ROW_TILE)
    (hmb,) = _layer_norm(meta_tokens, ln_in_g, ln_in_b, BF16, N_META, False)

    rope_q = functools.partial(_rope_compute, rope_cols=ATTN_WIDTH, scale=ATTN_SCALE)
    rope_kv = functools.partial(_rope_compute, rope_cols=KV_WIDTH, scale=1.0)
    wide, narrow = COL_TILE, STAGE_COLS
    q = _staged_matmul(rope_q, [h0b], [(w_in, q0)], [tab_x], [_rope_spec(tab_x, ROW_TILE)], BF16, ATTN_WIDTH,
                       ROW_TILE, wide, "proj_q")
    kv_meta = _staged_matmul(rope_kv, [hmb], [(w_in, k0)], [tab_m], [_rope_spec(tab_m, N_META)], BF16,
                             2 * KV_WIDTH, N_META, narrow, "proj_kv_meta")
    u = _staged_matmul(_plain_compute, [h0b], [(w_in, u0)], [], [], F32, POOL_WIDTH, ROW_TILE, wide, "proj_u")
    u_meta = _staged_matmul(_plain_compute, [hmb], [(w_in, u0)], [], [], F32, POOL_WIDTH, N_META, narrow,
                            "proj_u_meta")
    gates = _staged_matmul(_gate_compute, [h0b], [(w_in, g0)], [b_gate[0].reshape(1, 2 * d)],
                           [pl.BlockSpec((1, wide), lambda j, i: (0, _tile_col(j, i)))], BF16, 2 * d,
                           ROW_TILE, wide, "proj_gate")

    attn = _attention(q, kv, kv_meta, attn_sinks[0], batch, seq)
    pool = _pool(u, u_meta, w_pool_grp[0].astype(BF16), pool_scale[0].reshape(1, POOL_WIDTH), batch, seq,
                 POOL_ROW_TILE)

    gate_b0 = d // wide
    mixed = _staged_matmul(
        _up_compute, [attn, pool], [(w_attn_up, 0), (w_pool_up, 0)], [gates, gates],
        [pl.BlockSpec((ROW_TILE, wide), lambda j, i: (_tile_row(j, i), _tile_col(j, i))),
         pl.BlockSpec((ROW_TILE, wide), lambda j, i: (_tile_row(j, i), _tile_col(j, i) + gate_b0))],
        BF16, d, ROW_TILE, wide, "up_merge")
    y1 = _staged_matmul(
        _resid_compute, [mixed], [(w_out, 0)], [x2, stat0, jnp.stack([ln_in_g, ln_in_b])],
        _resid_specs(ROW_TILE, wide, _tile_row, _tile_col), F32, d, ROW_TILE, wide, "out_proj")
    h1b, stat1 = _layer_norm(y1, ln1_g[0], ln1_b[0], BF16, LN_ROW_TILE, True)

    act = _staged_matmul(_swiglu_compute, [h1b], [(w_ffn_in, 0), (w_ffn_in, D_FF)], [], [], BF16, D_FF,
                         FFN_ROW_TILE, FFN_COL_TILE, "swiglu_in")
    y2 = _staged_matmul(
        _resid_compute, [act], [(w_ffn_down, 0)], [y1, stat1, jnp.stack([ln1_g[0], ln1_b[0]])],
        _resid_specs(DOWN_ROW_TILE, DOWN_COL_TILE, _tile_row, _tile_col), F32, d, DOWN_ROW_TILE, DOWN_COL_TILE,
        "ffn_down")
    (out,) = _layer_norm(y2, ln2_g[0], ln2_b[0], F32, LN_ROW_TILE, False)
    return out.reshape(batch, seq, d)
```
